```python
import math
import jax, jax.numpy as jnp
from jax import lax
import numpy as np

D_MODEL = 4096
BATCH = 1
SEQ = 8192
DEPTH = 1

MEM_TOKENS = 256
HEAD_DIM = 128
MIX_WIDTH = D_MODEL
DIFF_HEADS = MIX_WIDTH // 2 // HEAD_DIM
DIFF_QK_DIM = HEAD_DIM // 2
DIFF_V_DIM = HEAD_DIM
SB_HEADS = MIX_WIDTH // 2 // HEAD_DIM
SB_DIM = HEAD_DIM
DIFF_WIDTH = DIFF_HEADS * DIFF_V_DIM
SB_WIDTH = SB_HEADS * SB_DIM
IN_WIDTH = 3 * DIFF_WIDTH + 3 * SB_WIDTH
XATTN_HEADS = 4
XATTN_DIM = HEAD_DIM
XATTN_WIDTH = XATTN_HEADS * XATTN_DIM
D_FF = 7 * D_MODEL // 2
ROPE_THETA = 500000.0
ROT_DIM = DIFF_QK_DIM // 4
Q_BLOCK = 128
NORM_EPS = 1e-6

kernel_name = "hymba_diff_stickbreaking_macaron_sandwich"


def rmsnorm(x, gain):
    xf = x.astype(jnp.float32)
    xf = xf * lax.rsqrt(jnp.mean(xf * xf, axis=-1, keepdims=True) + NORM_EPS)
    return xf.astype(x.dtype) * gain


def swiglu(x, w_gate, w_up, w_down):
    return (jax.nn.silu(x @ w_gate) * (x @ w_up)) @ w_down


def rope_cos_sin(positions):
    inv_freq = ROPE_THETA ** (-jnp.arange(0, ROT_DIM, 2, dtype=jnp.float32) / ROT_DIM)
    ang = positions.astype(jnp.float32)[..., None] * inv_freq
    return jnp.cos(ang)[:, None], jnp.sin(ang)[:, None]


def partial_rope(x, cos, sin):
    half = ROT_DIM // 2
    x1 = x[..., :half].astype(jnp.float32)
    x2 = x[..., half:ROT_DIM].astype(jnp.float32)
    rot = jnp.concatenate([x1 * cos - x2 * sin, x2 * cos + x1 * sin], axis=-1).astype(x.dtype)
    return jnp.concatenate([rot, x[..., ROT_DIM:]], axis=-1)


def to_blocks(q):
    b, h, s, d = q.shape
    return jnp.moveaxis(q.reshape(b, h, s // Q_BLOCK, Q_BLOCK, d), 2, 0)


def from_blocks(o):
    nb, b, h, qb, d = o.shape
    return jnp.moveaxis(o, 0, 2).reshape(b, h, nb * qb, d)


def diff_attention(q1, q2, k1, k2, v, lam):
    s_len = k1.shape[2]
    scale = DIFF_QK_DIM ** -0.5
    kpos = jnp.arange(s_len)

    def one_block(args):
        q1b, q2b, blk = args
        qpos = blk * Q_BLOCK + jnp.arange(Q_BLOCK)
        causal = kpos[None, :] <= qpos[:, None]

        def probs(qb, k):
            sc = jnp.einsum('bhqd,bhkd->bhqk', qb, k).astype(jnp.float32) * scale
            return jax.nn.softmax(jnp.where(causal, sc, -jnp.inf), axis=-1)

        w = probs(q1b, k1) - lam * probs(q2b, k2)
        return jnp.einsum('bhqk,bhkd->bhqd', w.astype(v.dtype), v)

    nb = s_len // Q_BLOCK
    out = lax.map(one_block, (to_blocks(q1), to_blocks(q2), jnp.arange(nb)))
    return from_blocks(out)


def stick_breaking_attention(q, k, v):
    s_len = k.shape[2]
    scale = SB_DIM ** -0.5
    kpos = jnp.arange(s_len)

    def one_block(args):
        qb, blk = args
        qpos = blk * Q_BLOCK + jnp.arange(Q_BLOCK)
        strict = kpos[None, :] < qpos[:, None]
        z = jnp.einsum('bhqd,bhkd->bhqk', qb, k).astype(jnp.float32) * scale
        log_1m_beta = jnp.where(strict, jax.nn.log_sigmoid(-z), 0.0)
        suffix = lax.cumsum(log_1m_beta, axis=3, reverse=True) - log_1m_beta
        a = jnp.where(strict, jnp.exp(jax.nn.log_sigmoid(z) + suffix), 0.0)
        return jnp.einsum('bhqk,bhkd->bhqd', a.astype(v.dtype), v)

    nb = s_len // Q_BLOCK
    out = lax.map(one_block, (to_blocks(q), jnp.arange(nb)))
    return from_blocks(out)


def token_mixer(hn, positions, w_in, w_out, lambda_q1, lambda_k1, lambda_q2, lambda_k2,
                diff_subln, sb_norm, lambda_init):
    b, s, _ = hn.shape
    proj = hn @ w_in
    dq, dk, dv, sq, sk, sv = jnp.split(proj, 6, axis=-1)
    heads = lambda t, h, d: t.reshape(b, s, h, d).transpose(0, 2, 1, 3)

    cos, sin = rope_cos_sin(positions)
    dq = dq.reshape(b, s, DIFF_HEADS, 2, DIFF_QK_DIM).transpose(0, 2, 1, 3, 4)
    dk = dk.reshape(b, s, DIFF_HEADS, 2, DIFF_QK_DIM).transpose(0, 2, 1, 3, 4)
    q1 = partial_rope(dq[..., 0, :], cos, sin)
    q2 = partial_rope(dq[..., 1, :], cos, sin)
    k1 = partial_rope(dk[..., 0, :], cos, sin)
    k2 = partial_rope(dk[..., 1, :], cos, sin)
    vd = heads(dv, DIFF_HEADS, DIFF_V_DIM)
    lam = (jnp.exp(jnp.sum(lambda_q1.astype(jnp.float32) * lambda_k1.astype(jnp.float32)))
           - jnp.exp(jnp.sum(lambda_q2.astype(jnp.float32) * lambda_k2.astype(jnp.float32)))
           + lambda_init)
    diff_out = diff_attention(q1, q2, k1, k2, vd, lam)
    diff_out = rmsnorm(diff_out, diff_subln) * (1.0 - lambda_init)

    sb_out = stick_breaking_attention(heads(sq, SB_HEADS, SB_DIM), heads(sk, SB_HEADS, SB_DIM),
                                      heads(sv, SB_HEADS, SB_DIM))
    sb_out = rmsnorm(sb_out, sb_norm)

    merged = jnp.concatenate([
        diff_out.transpose(0, 2, 1, 3).reshape(b, s, DIFF_WIDTH),
        sb_out.transpose(0, 2, 1, 3).reshape(b, s, SB_WIDTH)], axis=-1)
    return merged @ w_out


def memory_cross_attention(hn, mem_n, w_q, w_kv, w_o):
    b, s, _ = hn.shape
    q = (hn @ w_q).reshape(b, s, XATTN_HEADS, XATTN_DIM)
    k, v = jnp.split((mem_n @ w_kv).reshape(b, MEM_TOKENS, 2, XATTN_HEADS, XATTN_DIM), 2, axis=2)
    k, v = k[:, :, 0], v[:, :, 0]
    sc = jnp.einsum('bshd,bmhd->bhsm', q, k).astype(jnp.float32) * XATTN_DIM ** -0.5
    p = jax.nn.softmax(sc, axis=-1).astype(v.dtype)
    o = jnp.einsum('bhsm,bmhd->bshd', p, v).reshape(b, s, XATTN_WIDTH)
    return o @ w_o


def setup_inputs(seed: int = 0) -> dict:
    key = jax.random.key(seed)
    ks = jax.random.split(key, 32)
    L = DEPTH
    nrm = lambda k, shape, fan_in: jax.random.normal(k, shape, jnp.float32) * fan_in ** -0.5
    gain = lambda k, d: 1.0 + 0.02 * jax.random.normal(k, (L, d), jnp.float32)
    offsets = jax.random.randint(ks[2], (BATCH, 1), 0, 4096, dtype=jnp.int32)
    positions = offsets + jnp.arange(SEQ, dtype=jnp.int32)[None, :]
    return {
        "x": jax.random.normal(ks[0], (BATCH, SEQ, D_MODEL), jnp.float32),
        "mem": jax.random.normal(ks[1], (BATCH, MEM_TOKENS, D_MODEL), jnp.float32),
        "positions": positions,
        "ffn1_norm_pre": gain(ks[3], D_MODEL),
        "ffn1_norm_post": gain(ks[4], D_MODEL),
        "ffn1_w_gate": nrm(ks[5], (L, D_MODEL, D_FF), D_MODEL),
        "ffn1_w_up": nrm(ks[6], (L, D_MODEL, D_FF), D_MODEL),
        "ffn1_w_down": nrm(ks[7], (L, D_FF, D_MODEL), D_FF),
        "mix_norm_pre": gain(ks[8], D_MODEL),
        "mix_norm_post": gain(ks[9], D_MODEL),
        "w_in": nrm(ks[10], (L, D_MODEL, IN_WIDTH), D_MODEL),
        "w_out": nrm(ks[11], (L, MIX_WIDTH, D_MODEL), MIX_WIDTH),
        "lambda_q1": 0.1 * jax.random.normal(ks[12], (L, DIFF_QK_DIM), jnp.float32),
        "lambda_k1": 0.1 * jax.random.normal(ks[13], (L, DIFF_QK_DIM), jnp.float32),
        "lambda_q2": 0.1 * jax.random.normal(ks[14], (L, DIFF_QK_DIM), jnp.float32),
        "lambda_k2": 0.1 * jax.random.normal(ks[15], (L, DIFF_QK_DIM), jnp.float32),
        "diff_subln": gain(ks[16], DIFF_V_DIM),
        "sb_norm": gain(ks[17], SB_DIM),
        "xattn_norm_pre": gain(ks[18], D_MODEL),
        "xattn_norm_post": gain(ks[19], D_MODEL),
        "mem_norm": gain(ks[20], D_MODEL),
        "xattn_w_q": nrm(ks[21], (L, D_MODEL, XATTN_WIDTH), D_MODEL),
        "xattn_w_kv": nrm(ks[22], (L, D_MODEL, 2 * XATTN_WIDTH), D_MODEL),
        "xattn_w_o": nrm(ks[23], (L, XATTN_WIDTH, D_MODEL), XATTN_WIDTH),
        "ffn2_norm_pre": gain(ks[24], D_MODEL),
        "ffn2_norm_post": gain(ks[25], D_MODEL),
        "ffn2_w_gate": nrm(ks[26], (L, D_MODEL, D_FF), D_MODEL),
        "ffn2_w_up": nrm(ks[27], (L, D_MODEL, D_FF), D_MODEL),
        "ffn2_w_down": nrm(ks[28], (L, D_FF, D_MODEL), D_FF),
    }


def reference(x, mem, positions, ffn1_norm_pre, ffn1_norm_post, ffn1_w_gate, ffn1_w_up,
              ffn1_w_down, mix_norm_pre, mix_norm_post, w_in, w_out, lambda_q1, lambda_k1,
              lambda_q2, lambda_k2, diff_subln, sb_norm, xattn_norm_pre, xattn_norm_post,
              mem_norm, xattn_w_q, xattn_w_kv, xattn_w_o, ffn2_norm_pre, ffn2_norm_post,
              ffn2_w_gate, ffn2_w_up, ffn2_w_down):
    h = x
    for l in range(DEPTH):
        lambda_init = 0.8 - 0.6 * math.exp(-0.3 * l)
        y = swiglu(rmsnorm(h, ffn1_norm_pre[l]), ffn1_w_gate[l], ffn1_w_up[l], ffn1_w_down[l])
        h = h + 0.5 * rmsnorm(y, ffn1_norm_post[l])
        y = token_mixer(rmsnorm(h, mix_norm_pre[l]), positions, w_in[l], w_out[l],
                        lambda_q1[l], lambda_k1[l], lambda_q2[l], lambda_k2[l],
                        diff_subln[l], sb_norm[l], lambda_init)
        h = h + rmsnorm(y, mix_norm_post[l])
        y = memory_cross_attention(rmsnorm(h, xattn_norm_pre[l]), rmsnorm(mem, mem_norm[l]),
                                   xattn_w_q[l], xattn_w_kv[l], xattn_w_o[l])
        h = h + rmsnorm(y, xattn_norm_post[l])
        y = swiglu(rmsnorm(h, ffn2_norm_pre[l]), ffn2_w_gate[l], ffn2_w_up[l], ffn2_w_down[l])
        h = h + 0.5 * rmsnorm(y, ffn2_norm_post[l])
    return h
```

```python
import functools
import math

import jax
import jax.numpy as jnp
from jax import lax
from jax.experimental import pallas as pl
from jax.experimental.pallas import tpu as pltpu

F32 = jnp.float32
BF16 = jnp.bfloat16

HEAD_DIM = 128
DIFF_QK_DIM = 64
ROT_DIM = 16
ROPE_THETA = 500000.0
NORM_EPS = 1e-6
XATTN_HEADS = 4
LANES = 128

VMEM_LIMIT_BYTES = 56 * 1024 * 1024
SB_EXIT = 100.0


def _params(*sem):
    return pltpu.CompilerParams(dimension_semantics=sem, vmem_limit_bytes=VMEM_LIMIT_BYTES)


def _rms(x):
    return x * lax.rsqrt(jnp.mean(x * x, axis=-1, keepdims=True) + NORM_EPS)


def _prenorm_kernel(x_ref, g_ref, o_ref):
    o_ref[...] = (_rms(x_ref[...]) * g_ref[...]).astype(o_ref.dtype)


def prenorm(x, gain, *, tm=256):
    m, d = x.shape
    return pl.pallas_call(
        _prenorm_kernel,
        grid=(m // tm,),
        in_specs=[pl.BlockSpec((tm, d), lambda i: (i, 0)), pl.BlockSpec((1, d), lambda i: (0, 0))],
        out_specs=pl.BlockSpec((tm, d), lambda i: (i, 0)),
        out_shape=jax.ShapeDtypeStruct((m, d), BF16),
        compiler_params=_params("parallel"),
        name="prenorm",
    )(x, gain.reshape(1, d))


def _post_pre_kernel(h_ref, y_ref, gpost_ref, gpre_ref, hout_ref, hn_ref, *, coef):
    h = h_ref[...] + coef * (_rms(y_ref[...]) * gpost_ref[...])
    hout_ref[...] = h
    hn_ref[...] = (_rms(h) * gpre_ref[...]).astype(hn_ref.dtype)


def _post_kernel(h_ref, y_ref, gpost_ref, hout_ref, *, coef):
    hout_ref[...] = h_ref[...] + coef * (_rms(y_ref[...]) * gpost_ref[...])


def post_pre(h, y, gpost, gpre, coef, *, tm=256):
    m, d = h.shape
    row = pl.BlockSpec((tm, d), lambda i: (i, 0))
    vec = pl.BlockSpec((1, d), lambda i: (0, 0))
    if gpre is None:
        return pl.pallas_call(
            functools.partial(_post_kernel, coef=coef),
            grid=(m // tm,),
            in_specs=[row, row, vec],
            out_specs=row,
            out_shape=jax.ShapeDtypeStruct((m, d), F32),
            compiler_params=_params("parallel"),
            name="post_norm",
        )(h, y, gpost.reshape(1, d))
    return pl.pallas_call(
        functools.partial(_post_pre_kernel, coef=coef),
        grid=(m // tm,),
        in_specs=[row, row, vec, vec],
        out_specs=[row, row],
        out_shape=[jax.ShapeDtypeStruct((m, d), F32), jax.ShapeDtypeStruct((m, d), BF16)],
        compiler_params=_params("parallel"),
        name="post_pre_norm",
    )(h, y, gpost.reshape(1, d), gpre.reshape(1, d))


def _ffn_up_kernel(x_ref, wg_ref, wu_ref, o_ref):
    x = x_ref[...]
    g = jnp.dot(x, wg_ref[...], preferred_element_type=F32)
    u = jnp.dot(x, wu_ref[...], preferred_element_type=F32)
    o_ref[...] = (g * jax.nn.sigmoid(g) * u).astype(o_ref.dtype)


def ffn_up(xn, wg, wu, *, tm=1024, tn=512):
    m, k = xn.shape
    n = wg.shape[1]
    tm, tn = min(tm, m), min(tn, n)
    return pl.pallas_call(
        _ffn_up_kernel,
        grid=(m // tm, n // tn),
        in_specs=[
            pl.BlockSpec((tm, k), lambda i, j: (i, 0)),
            pl.BlockSpec((k, tn), lambda i, j: (0, j)),
            pl.BlockSpec((k, tn), lambda i, j: (0, j)),
        ],
        out_specs=pl.BlockSpec((tm, tn), lambda i, j: (i, j)),
        out_shape=jax.ShapeDtypeStruct((m, n), BF16),
        compiler_params=_params("parallel", "arbitrary"),
        name="ffn_up",
    )(xn, wg, wu)


def _mm_acc_kernel(a_ref, b_ref, o_ref):
    @pl.when(pl.program_id(2) == 0)
    def _():
        o_ref[...] = jnp.zeros_like(o_ref)

    o_ref[...] += jnp.dot(a_ref[...], b_ref[...], preferred_element_type=F32)


def matmul_kblocked(a, b, *, tm=1024, tn=1024, tk=2048):
    m, k = a.shape
    n = b.shape[1]
    tm, tn, tk = min(tm, m), min(tn, n), min(tk, k)
    return pl.pallas_call(
        _mm_acc_kernel,
        grid=(m // tm, n // tn, k // tk),
        in_specs=[
            pl.BlockSpec((tm, tk), lambda i, j, l: (i, l)),
            pl.BlockSpec((tk, tn), lambda i, j, l: (l, j)),
        ],
        out_specs=pl.BlockSpec((tm, tn), lambda i, j, l: (i, j)),
        out_shape=jax.ShapeDtypeStruct((m, n), F32),
        compiler_params=_params("parallel", "parallel", "arbitrary"),
        name="matmul_kblocked",
    )(a, b)


def _out_proj_kernel(a_ref, b_ref, wa_ref, wb_ref, o_ref):
    o_ref[...] = jnp.dot(a_ref[...], wa_ref[...], preferred_element_type=F32) + jnp.dot(
        b_ref[...], wb_ref[...], preferred_element_type=F32
    )


def out_proj(a, b, w, *, tm=1024, tn=1024):
    m, ka = a.shape
    kb = b.shape[1]
    n = w.shape[1]
    tm, tn = min(tm, m), min(tn, n)
    assert ka == kb
    return pl.pallas_call(
        _out_proj_kernel,
        grid=(m // tm, n // tn),
        in_specs=[
            pl.BlockSpec((tm, ka), lambda i, j: (i, 0)),
            pl.BlockSpec((tm, kb), lambda i, j: (i, 0)),
            pl.BlockSpec((ka, tn), lambda i, j: (0, j)),
            pl.BlockSpec((kb, tn), lambda i, j: (1, j)),
        ],
        out_specs=pl.BlockSpec((tm, tn), lambda i, j: (i, j)),
        out_shape=jax.ShapeDtypeStruct((m, n), F32),
        compiler_params=_params("parallel", "arbitrary"),
        name="out_proj",
    )(a, b, w, w)


def _rope_table_kernel(pos_ref, invf_ref, ct_ref, s1_ref, s2_ref):
    ang = pos_ref[...] * invf_ref[...]
    cos, sin = jnp.cos(ang), jnp.sin(ang)
    lane = lax.broadcasted_iota(jnp.int32, ang.shape, 1) % DIFF_QK_DIM
    half = ROT_DIM // 2
    ct_ref[...] = jnp.where(lane < ROT_DIM, cos, 1.0)
    s1_ref[...] = jnp.where(lane < half, -sin, 0.0)
    s2_ref[...] = jnp.where((lane >= half) & (lane < ROT_DIM), sin, 0.0)


def rope_tables(positions_col, *, tm=1024):
    s = positions_col.shape[0]
    tm = min(tm, s)
    half = ROT_DIM // 2
    inv_freq = ROPE_THETA ** (-jnp.arange(0, ROT_DIM, 2, dtype=F32) / ROT_DIM)
    invf = jnp.tile(inv_freq, LANES // half).reshape(1, LANES)
    tab = pl.BlockSpec((tm, LANES), lambda i: (i, 0))
    return pl.pallas_call(
        _rope_table_kernel,
        grid=(s // tm,),
        in_specs=[pl.BlockSpec((tm, 1), lambda i: (i, 0)), pl.BlockSpec((1, LANES), lambda i: (0, 0))],
        out_specs=[tab, tab, tab],
        out_shape=[jax.ShapeDtypeStruct((s, LANES), F32)] * 3,
        compiler_params=_params("parallel"),
        name="rope_tables",
    )(positions_col, invf)


def _in_proj_kernel(x_ref, w_ref, ct_ref, s1_ref, s2_ref, o_ref, *, n_rope_tiles, n_scaled_tiles, q_scale):
    j = pl.program_id(1)
    acc = jnp.dot(x_ref[...], w_ref[...], preferred_element_type=F32)

    @pl.when(j >= n_rope_tiles)
    def _():
        o_ref[...] = acc.astype(o_ref.dtype)

    @pl.when(j < n_rope_tiles)
    def _():
        scale = jnp.where(j < n_scaled_tiles, q_scale, 1.0).astype(F32)
        ct, s1, s2 = ct_ref[...], s1_ref[...], s2_ref[...]
        half = ROT_DIM // 2
        for c in range(acc.shape[1] // LANES):
            a = acc[:, c * LANES:(c + 1) * LANES]
            r = a * ct + pltpu.roll(a, LANES - half, 1) * s1 + pltpu.roll(a, half, 1) * s2
            o_ref[:, c * LANES:(c + 1) * LANES] = (r * scale).astype(o_ref.dtype)


def in_proj(xn, w, ct, s1, s2, *, rope_cols, scaled_cols, q_scale, tm=1024, tn=1024):
    m, k = xn.shape
    n = w.shape[1]
    tm, tn = min(tm, m), min(tn, n, scaled_cols)
    assert rope_cols % tn == 0 and scaled_cols % tn == 0
    tab = pl.BlockSpec((tm, LANES), lambda i, j: (i, 0))
    return pl.pallas_call(
        functools.partial(_in_proj_kernel, n_rope_tiles=rope_cols // tn, n_scaled_tiles=scaled_cols // tn,
                          q_scale=q_scale),
        grid=(m // tm, n // tn),
        in_specs=[
            pl.BlockSpec((tm, k), lambda i, j: (i, 0)),
            pl.BlockSpec((k, tn), lambda i, j: (0, j)),
            tab, tab, tab,
        ],
        out_specs=pl.BlockSpec((tm, tn), lambda i, j: (i, j)),
        out_shape=jax.ShapeDtypeStruct((m, n), BF16),
        compiler_params=_params("parallel", "arbitrary"),
        name="in_proj",
    )(xn, w, ct, s1, s2)


def _qk(q, k):
    return lax.dot_general(q, k, (((1,), (1,)), ((), ())), preferred_element_type=F32)


def _diff_attn_kernel(q_ref, k_ref, v_ref, lq1_ref, lk1_ref, lq2_ref, lk2_ref, g_ref, o_ref,
                      m_ref, l_ref, acc_ref, *, tq, tk, lambda_init):
    i = pl.program_id(1)
    q = q_ref[...]
    lane = lax.broadcasted_iota(jnp.int32, q.shape, 1)
    zero = jnp.zeros_like(q)
    qs = (jnp.where(lane < DIFF_QK_DIM, q, zero), jnp.where(lane >= DIFF_QK_DIM, q, zero))

    m_ref[...] = jnp.full_like(m_ref, -jnp.inf)
    l_ref[...] = jnp.zeros_like(l_ref)
    acc_ref[...] = jnp.zeros_like(acc_ref)

    def block(kb, masked):
        start = pl.multiple_of(kb * tk, tk)
        k = k_ref[pl.ds(start, tk), :]
        v = v_ref[pl.ds(start, tk), :]
        for c in range(2):
            s = _qk(qs[c], k)
            if masked:
                rows = lax.broadcasted_iota(jnp.int32, s.shape, 0)
                cols = lax.broadcasted_iota(jnp.int32, s.shape, 1)
                s = jnp.where(cols <= rows, s, -jnp.inf)
            m_old = m_ref[c]
            m_new = jnp.maximum(m_old, jnp.max(s, axis=-1, keepdims=True))
            alpha = jnp.exp(m_old - m_new)
            p = jnp.exp(s - m_new)
            l_ref[c] = alpha * l_ref[c] + jnp.sum(p, axis=-1, keepdims=True)
            acc_ref[c] = alpha * acc_ref[c] + jnp.dot(p.astype(BF16), v, preferred_element_type=F32)
            m_ref[c] = m_new

    def body(kb, carry):
        block(kb, False)
        return carry

    lax.fori_loop(0, i, body, 0)
    block(i, True)

    lam = (jnp.exp(jnp.sum(lq1_ref[...] * lk1_ref[...], keepdims=True))
           - jnp.exp(jnp.sum(lq2_ref[...] * lk2_ref[...], keepdims=True)) + lambda_init)
    out = acc_ref[0] / l_ref[0] - lam * (acc_ref[1] / l_ref[1])
    o_ref[...] = ((_rms(out) * g_ref[...]) * (1.0 - lambda_init)).astype(o_ref.dtype)


def diff_attention(proj, lq1, lk1, lq2, lk2, subln, *, n_heads, q_col, k_col, v_col, lambda_init, tq=256):
    s = proj.shape[0]
    tq = min(tq, s)
    qb, kb, vb = q_col // HEAD_DIM, k_col // HEAD_DIM, v_col // HEAD_DIM
    lam_spec = pl.BlockSpec((1, DIFF_QK_DIM), lambda h, i: (0, 0))
    return pl.pallas_call(
        functools.partial(_diff_attn_kernel, tq=tq, tk=tq, lambda_init=lambda_init),
        grid=(n_heads, s // tq),
        in_specs=[
            pl.BlockSpec((tq, HEAD_DIM), lambda h, i: (i, qb + h)),
            pl.BlockSpec((s, HEAD_DIM), lambda h, i: (0, kb + h)),
            pl.BlockSpec((s, HEAD_DIM), lambda h, i: (0, vb + h)),
            lam_spec, lam_spec, lam_spec, lam_spec,
            pl.BlockSpec((1, HEAD_DIM), lambda h, i: (0, 0)),
        ],
        out_specs=pl.BlockSpec((tq, HEAD_DIM), lambda h, i: (i, h)),
        out_shape=jax.ShapeDtypeStruct((s, n_heads * HEAD_DIM), BF16),
        scratch_shapes=[
            pltpu.VMEM((2, tq, 1), F32),
            pltpu.VMEM((2, tq, 1), F32),
            pltpu.VMEM((2, tq, HEAD_DIM), F32),
        ],
        compiler_params=_params("parallel", "arbitrary"),
        name="diff_attention",
    )(proj, proj, proj, lq1.reshape(1, -1), lk1.reshape(1, -1), lq2.reshape(1, -1), lk2.reshape(1, -1),
      subln.reshape(1, -1))


def _sb_attn_kernel(q_ref, k_ref, v_ref, tri_ref, g_ref, o_ref, c_ref, acc_ref, *, tq, tk, scale):
    i = pl.program_id(1)
    q = q_ref[...]
    tri = tri_ref[...]
    c_ref[...] = jnp.zeros_like(c_ref)
    acc_ref[...] = jnp.zeros_like(acc_ref)

    def block(kb, masked):
        start = pl.multiple_of(kb * tk, tk)
        k = k_ref[pl.ds(start, tk), :]
        v = v_ref[pl.ds(start, tk), :]
        z = _qk(q, k) * scale
        lsm = -(jnp.maximum(z, 0.0) + jnp.log(1.0 + jnp.exp(-jnp.abs(z))))
        if masked:
            rows = lax.broadcasted_iota(jnp.int32, z.shape, 0)
            cols = lax.broadcasted_iota(jnp.int32, z.shape, 1)
            strict = cols < rows
            lsm = jnp.where(strict, lsm, 0.0)
        hi = lsm.astype(BF16)
        lo = (lsm - hi.astype(F32)).astype(BF16)
        r = jnp.dot(hi, tri, preferred_element_type=F32) + jnp.dot(lo, tri, preferred_element_type=F32)
        c = c_ref[...]
        a = jnp.exp(z + r + c)
        if masked:
            a = jnp.where(strict, a, 0.0)
        acc_ref[...] += jnp.dot(a.astype(BF16), v, preferred_element_type=F32)
        c_new = c + r[:, 0:1]
        c_ref[...] = c_new
        return jnp.max(c_new)

    cmax = block(i, True)

    def cond(carry):
        kb, cm = carry
        return jnp.logical_and(kb >= 0, cm > -SB_EXIT)

    def body(carry):
        kb, _ = carry
        return kb - 1, block(kb, False)

    lax.while_loop(cond, body, (i - 1, cmax))
    o_ref[...] = (_rms(acc_ref[...]) * g_ref[...]).astype(o_ref.dtype)


def sb_attention(proj, gain, *, n_heads, q_col, k_col, v_col, tq=256):
    s = proj.shape[0]
    tq = min(tq, s)
    tk = tq
    qb, kb, vb = q_col // HEAD_DIM, k_col // HEAD_DIM, v_col // HEAD_DIM
    tri = (lax.broadcasted_iota(jnp.int32, (tk, tk), 0) >= lax.broadcasted_iota(jnp.int32, (tk, tk), 1)).astype(BF16)
    return pl.pallas_call(
        functools.partial(_sb_attn_kernel, tq=tq, tk=tk, scale=HEAD_DIM ** -0.5),
        grid=(n_heads, s // tq),
        in_specs=[
            pl.BlockSpec((tq, HEAD_DIM), lambda h, i: (i, qb + h)),
            pl.BlockSpec((s, HEAD_DIM), lambda h, i: (0, kb + h)),
            pl.BlockSpec((s, HEAD_DIM), lambda h, i: (0, vb + h)),
            pl.BlockSpec((tk, tk), lambda h, i: (0, 0)),
            pl.BlockSpec((1, HEAD_DIM), lambda h, i: (0, 0)),
        ],
        out_specs=pl.BlockSpec((tq, HEAD_DIM), lambda h, i: (i, h)),
        out_shape=jax.ShapeDtypeStruct((s, n_heads * HEAD_DIM), BF16),
        scratch_shapes=[pltpu.VMEM((tq, 1), F32), pltpu.VMEM((tq, HEAD_DIM), F32)],
        compiler_params=_params("parallel", "arbitrary"),
        name="sb_attention",
    )(proj, proj, proj, tri, gain.reshape(1, -1))


def _mem_kv_kernel(mem_ref, g_ref, w_ref, o_ref):
    mn = (_rms(mem_ref[...]) * g_ref[...]).astype(BF16)
    o_ref[...] = jnp.dot(mn, w_ref[...], preferred_element_type=F32).astype(o_ref.dtype)


def mem_kv(mem, gain, w_kv):
    mt, d = mem.shape
    n = w_kv.shape[1]
    return pl.pallas_call(
        _mem_kv_kernel,
        grid=(1,),
        in_specs=[pl.BlockSpec((mt, d), lambda i: (0, 0)), pl.BlockSpec((1, d), lambda i: (0, 0)),
                  pl.BlockSpec((d, n), lambda i: (0, 0))],
        out_specs=pl.BlockSpec((mt, n), lambda i: (0, 0)),
        out_shape=jax.ShapeDtypeStruct((mt, n), BF16),
        compiler_params=_params("arbitrary"),
        name="mem_kv",
    )(mem, gain.reshape(1, d), w_kv)


def _xattn_kernel(h_ref, gpre_ref, wq_ref, kv_ref, wo_ref, gpost_ref, gnext_ref, hout_ref, hn_ref, *, scale):
    h = h_ref[...]
    hn = (_rms(h) * gpre_ref[...]).astype(BF16)
    q = jnp.dot(hn, wq_ref[...], preferred_element_type=F32).astype(BF16)
    width = XATTN_HEADS * HEAD_DIM
    outs = []
    for hd in range(XATTN_HEADS):
        cols = slice(hd * HEAD_DIM, (hd + 1) * HEAD_DIM)
        k = kv_ref[:, cols]
        v = kv_ref[:, width + hd * HEAD_DIM:width + (hd + 1) * HEAD_DIM]
        sc = _qk(q[:, cols], k) * scale
        sc = sc - jnp.max(sc, axis=-1, keepdims=True)
        e = jnp.exp(sc)
        p = e / jnp.sum(e, axis=-1, keepdims=True)
        outs.append(jnp.dot(p.astype(BF16), v, preferred_element_type=F32).astype(BF16))
    o = jnp.concatenate(outs, axis=-1)
    y = jnp.dot(o, wo_ref[...], preferred_element_type=F32)
    h2 = h + _rms(y) * gpost_ref[...]
    hout_ref[...] = h2
    hn_ref[...] = (_rms(h2) * gnext_ref[...]).astype(hn_ref.dtype)


def xattn_block(h, gpre, wq, kv, wo, gpost, gnext, *, tm=256):
    m, d = h.shape
    tm = min(tm, m)
    width = wq.shape[1]
    row = pl.BlockSpec((tm, d), lambda i: (i, 0))
    vec = pl.BlockSpec((1, d), lambda i: (0, 0))
    return pl.pallas_call(
        functools.partial(_xattn_kernel, scale=HEAD_DIM ** -0.5),
        grid=(m // tm,),
        in_specs=[row, vec, pl.BlockSpec((d, width), lambda i: (0, 0)),
                  pl.BlockSpec(kv.shape, lambda i: (0, 0)), pl.BlockSpec((width, d), lambda i: (0, 0)), vec, vec],
        out_specs=[row, row],
        out_shape=[jax.ShapeDtypeStruct((m, d), F32), jax.ShapeDtypeStruct((m, d), BF16)],
        compiler_params=_params("parallel"),
        name="xattn_block",
    )(h, gpre.reshape(1, d), wq, kv, wo, gpost.reshape(1, d), gnext.reshape(1, d))


def kernel(x, mem, positions, ffn1_norm_pre, ffn1_norm_post, ffn1_w_gate, ffn1_w_up, ffn1_w_down, mix_norm_pre, mix_norm_post, w_in, w_out, lambda_q1, lambda_k1, lambda_q2, lambda_k2, diff_subln, sb_norm, xattn_norm_pre, xattn_norm_post, mem_norm, xattn_w_q, xattn_w_kv, xattn_w_o, ffn2_norm_pre, ffn2_norm_post, ffn2_w_gate, ffn2_w_up, ffn2_w_down):
    b, s, d = x.shape
    depth = w_in.shape[0]
    in_width = w_in.shape[2]
    group = in_width // 6
    n_heads = group // HEAD_DIM
    outs = []
    for bi in range(b):
        h = x[bi]
        pos_col = positions[bi].astype(F32).reshape(s, 1)
        ct, s1, s2 = rope_tables(pos_col)
        hn = prenorm(h, ffn1_norm_pre[0])
        for l in range(depth):
            lambda_init = 0.8 - 0.6 * math.exp(-0.3 * l)
            cast = lambda w: w[l].astype(BF16)
            act = ffn_up(hn, cast(ffn1_w_gate), cast(ffn1_w_up))
            y = matmul_kblocked(act, cast(ffn1_w_down))
            h, hn = post_pre(h, y, ffn1_norm_post[l], mix_norm_pre[l], 0.5)
            proj = in_proj(hn, cast(w_in), ct, s1, s2, rope_cols=2 * group, scaled_cols=group,
                           q_scale=DIFF_QK_DIM ** -0.5)
            diff_out = diff_attention(proj, lambda_q1[l], lambda_k1[l], lambda_q2[l], lambda_k2[l], diff_subln[l],
                                      n_heads=n_heads, q_col=0, k_col=group, v_col=2 * group,
                                      lambda_init=lambda_init)
            sb_out = sb_attention(proj, sb_norm[l], n_heads=n_heads, q_col=3 * group, k_col=4 * group,
                                  v_col=5 * group)
            y = out_proj(diff_out, sb_out, cast(w_out))
            h = post_pre(h, y, mix_norm_post[l], None, 1.0)
            kv = mem_kv(mem[bi], mem_norm[l], cast(xattn_w_kv))
            h, hn = xattn_block(h, xattn_norm_pre[l], cast(xattn_w_q), kv, cast(xattn_w_o), xattn_norm_post[l],
                                ffn2_norm_pre[l])
            act = ffn_up(hn, cast(ffn2_w_gate), cast(ffn2_w_up))
            y = matmul_kblocked(act, cast(ffn2_w_down))
            if l + 1 < depth:
                h, hn = post_pre(h, y, ffn2_norm_post[l], ffn1_norm_pre[l + 1], 0.5)
            else:
                h = post_pre(h, y, ffn2_norm_post[l], None, 0.5)
        outs.append(h)
    return jnp.stack(outs, axis=0)
```

```python
import functools
import math

import jax
import jax.numpy as jnp
from jax import lax
from jax.experimental import pallas as pl
from jax.experimental.pallas import tpu as pltpu

F32 = jnp.float32
BF16 = jnp.bfloat16

HEAD_DIM = 128
DIFF_QK_DIM = 64
ROT_DIM = 16
ROPE_THETA = 500000.0
NORM_EPS = 1e-6
XATTN_HEADS = 4
LANES = 128

VMEM_LIMIT_BYTES = 56 * 1024 * 1024
SB_EXIT = 100.0


def _params(*sem):
    return pltpu.CompilerParams(dimension_semantics=sem, vmem_limit_bytes=VMEM_LIMIT_BYTES)


def _rms(x):
    return x * lax.rsqrt(jnp.mean(x * x, axis=-1, keepdims=True) + NORM_EPS)


def _prenorm_kernel(x_ref, g_ref, o_ref):
    o_ref[...] = (_rms(x_ref[...]) * g_ref[...]).astype(o_ref.dtype)


def prenorm(x, gain, *, tm=256):
    m, d = x.shape
    return pl.pallas_call(
        _prenorm_kernel,
        grid=(m // tm,),
        in_specs=[pl.BlockSpec((tm, d), lambda i: (i, 0)), pl.BlockSpec((1, d), lambda i: (0, 0))],
        out_specs=pl.BlockSpec((tm, d), lambda i: (i, 0)),
        out_shape=jax.ShapeDtypeStruct((m, d), BF16),
        compiler_params=_params("parallel"),
        name="prenorm",
    )(x, gain.reshape(1, d))


def _post_pre_kernel(h_ref, y_ref, gpost_ref, gpre_ref, hout_ref, hn_ref, *, coef):
    h = h_ref[...] + coef * (_rms(y_ref[...]) * gpost_ref[...])
    hout_ref[...] = h
    hn_ref[...] = (_rms(h) * gpre_ref[...]).astype(hn_ref.dtype)


def _post_kernel(h_ref, y_ref, gpost_ref, hout_ref, *, coef):
    hout_ref[...] = h_ref[...] + coef * (_rms(y_ref[...]) * gpost_ref[...])


def post_pre(h, y, gpost, gpre, coef, *, tm=256):
    m, d = h.shape
    row = pl.BlockSpec((tm, d), lambda i: (i, 0))
    vec = pl.BlockSpec((1, d), lambda i: (0, 0))
    if gpre is None:
        return pl.pallas_call(
            functools.partial(_post_kernel, coef=coef),
            grid=(m // tm,),
            in_specs=[row, row, vec],
            out_specs=row,
            out_shape=jax.ShapeDtypeStruct((m, d), F32),
            compiler_params=_params("parallel"),
            name="post_norm",
        )(h, y, gpost.reshape(1, d))
    return pl.pallas_call(
        functools.partial(_post_pre_kernel, coef=coef),
        grid=(m // tm,),
        in_specs=[row, row, vec, vec],
        out_specs=[row, row],
        out_shape=[jax.ShapeDtypeStruct((m, d), F32), jax.ShapeDtypeStruct((m, d), BF16)],
        compiler_params=_params("parallel"),
        name="post_pre_norm",
    )(h, y, gpost.reshape(1, d), gpre.reshape(1, d))


def _ffn_up_kernel(x_ref, wg_ref, wu_ref, o_ref):
    x = x_ref[...]
    g = jnp.dot(x, wg_ref[...], preferred_element_type=F32)
    u = jnp.dot(x, wu_ref[...], preferred_element_type=F32)
    o_ref[...] = (g * jax.nn.sigmoid(g) * u).astype(o_ref.dtype)


def ffn_up(xn, wg, wu, *, tm=1024, tn=512):
    m, k = xn.shape
    n = wg.shape[1]
    tm, tn = min(tm, m), min(tn, n)
    return pl.pallas_call(
        _ffn_up_kernel,
        grid=(m // tm, n // tn),
        in_specs=[
            pl.BlockSpec((tm, k), lambda i, j: (i, 0)),
            pl.BlockSpec((k, tn), lambda i, j: (0, j)),
            pl.BlockSpec((k, tn), lambda i, j: (0, j)),
        ],
        out_specs=pl.BlockSpec((tm, tn), lambda i, j: (i, j)),
        out_shape=jax.ShapeDtypeStruct((m, n), BF16),
        compiler_params=_params("parallel", "arbitrary"),
        name="ffn_up",
    )(xn, wg, wu)


def _mm_acc_kernel(a_ref, b_ref, o_ref):
    @pl.when(pl.program_id(2) == 0)
    def _():
        o_ref[...] = jnp.zeros_like(o_ref)

    o_ref[...] += jnp.dot(a_ref[...], b_ref[...], preferred_element_type=F32)


def matmul_kblocked(a, b, *, tm=1024, tn=1024, tk=2048):
    m, k = a.shape
    n = b.shape[1]
    tm, tn, tk = min(tm, m), min(tn, n), min(tk, k)
    return pl.pallas_call(
        _mm_acc_kernel,
        grid=(m // tm, n // tn, k // tk),
        in_specs=[
            pl.BlockSpec((tm, tk), lambda i, j, l: (i, l)),
            pl.BlockSpec((tk, tn), lambda i, j, l: (l, j)),
        ],
        out_specs=pl.BlockSpec((tm, tn), lambda i, j, l: (i, j)),
        out_shape=jax.ShapeDtypeStruct((m, n), F32),
        compiler_params=_params("parallel", "parallel", "arbitrary"),
        name="matmul_kblocked",
    )(a, b)


def _out_proj_kernel(a_ref, b_ref, wa_ref, wb_ref, o_ref):
    o_ref[...] = jnp.dot(a_ref[...], wa_ref[...], preferred_element_type=F32) + jnp.dot(
        b_ref[...], wb_ref[...], preferred_element_type=F32
    )


def out_proj(a, b, w, *, tm=1024, tn=1024):
    m, ka = a.shape
    kb = b.shape[1]
    n = w.shape[1]
    tm, tn = min(tm, m), min(tn, n)
    assert ka == kb
    return pl.pallas_call(
        _out_proj_kernel,
        grid=(m // tm, n // tn),
        in_specs=[
            pl.BlockSpec((tm, ka), lambda i, j: (i, 0)),
            pl.BlockSpec((tm, kb), lambda i, j: (i, 0)),
            pl.BlockSpec((ka, tn), lambda i, j: (0, j)),
            pl.BlockSpec((kb, tn), lambda i, j: (1, j)),
        ],
        out_specs=pl.BlockSpec((tm, tn), lambda i, j: (i, j)),
        out_shape=jax.ShapeDtypeStruct((m, n), F32),
        compiler_params=_params("parallel", "arbitrary"),
        name="out_proj",
    )(a, b, w, w)


def _rope_table_kernel(pos_ref, invf_ref, ct_ref, s1_ref, s2_ref):
    ang = pos_ref[...] * invf_ref[...]
    cos, sin = jnp.cos(ang), jnp.sin(ang)
    lane = lax.broadcasted_iota(jnp.int32, ang.shape, 1) % DIFF_QK_DIM
    half = ROT_DIM // 2
    ct_ref[...] = jnp.where(lane < ROT_DIM, cos, 1.0)
    s1_ref[...] = jnp.where(lane < half, -sin, 0.0)
    s2_ref[...] = jnp.where((lane >= half) & (lane < ROT_DIM), sin, 0.0)


def rope_tables(positions_col, *, tm=1024):
    s = positions_col.shape[0]
    tm = min(tm, s)
    half = ROT_DIM // 2
    inv_freq = ROPE_THETA ** (-jnp.arange(0, ROT_DIM, 2, dtype=F32) / ROT_DIM)
    invf = jnp.tile(inv_freq, LANES // half).reshape(1, LANES)
    tab = pl.BlockSpec((tm, LANES), lambda i: (i, 0))
    return pl.pallas_call(
        _rope_table_kernel,
        grid=(s // tm,),
        in_specs=[pl.BlockSpec((tm, 1), lambda i: (i, 0)), pl.BlockSpec((1, LANES), lambda i: (0, 0))],
        out_specs=[tab, tab, tab],
        out_shape=[jax.ShapeDtypeStruct((s, LANES), F32)] * 3,
        compiler_params=_params("parallel"),
        name="rope_tables",
    )(positions_col, invf)


def _in_proj_kernel(x_ref, w_ref, ct_ref, s1_ref, s2_ref, o_ref, *, n_rope_tiles, n_scaled_tiles, q_scale):
    j = pl.program_id(1)
    acc = jnp.dot(x_ref[...], w_ref[...], preferred_element_type=F32)

    @pl.when(j >= n_rope_tiles)
    def _():
        o_ref[...] = acc.astype(o_ref.dtype)

    @pl.when(j < n_rope_tiles)
    def _():
        scale = jnp.where(j < n_scaled_tiles, q_scale, 1.0).astype(F32)
        ct, s1, s2 = ct_ref[...], s1_ref[...], s2_ref[...]
        half = ROT_DIM // 2
        for c in range(acc.shape[1] // LANES):
            a = acc[:, c * LANES:(c + 1) * LANES]
            r = a * ct + pltpu.roll(a, LANES - half, 1) * s1 + pltpu.roll(a, half, 1) * s2
            o_ref[:, c * LANES:(c + 1) * LANES] = (r * scale).astype(o_ref.dtype)


def in_proj(xn, w, ct, s1, s2, *, rope_cols, scaled_cols, q_scale, tm=1024, tn=1024):
    m, k = xn.shape
    n = w.shape[1]
    tm, tn = min(tm, m), min(tn, n, scaled_cols)
    assert rope_cols % tn == 0 and scaled_cols % tn == 0
    tab = pl.BlockSpec((tm, LANES), lambda i, j: (i, 0))
    return pl.pallas_call(
        functools.partial(_in_proj_kernel, n_rope_tiles=rope_cols // tn, n_scaled_tiles=scaled_cols // tn,
                          q_scale=q_scale),
        grid=(m // tm, n // tn),
        in_specs=[
            pl.BlockSpec((tm, k), lambda i, j: (i, 0)),
            pl.BlockSpec((k, tn), lambda i, j: (0, j)),
            tab, tab, tab,
        ],
        out_specs=pl.BlockSpec((tm, tn), lambda i, j: (i, j)),
        out_shape=jax.ShapeDtypeStruct((m, n), BF16),
        compiler_params=_params("parallel", "arbitrary"),
        name="in_proj",
    )(xn, w, ct, s1, s2)


def _qk(q, k):
    return lax.dot_general(q, k, (((1,), (1,)), ((), ())), preferred_element_type=F32)


def _diff_attn_kernel(q_ref, k_ref, v_ref, lq1_ref, lk1_ref, lq2_ref, lk2_ref, g_ref, o_ref,
                      vp_ref, s_ref, p_ref, alpha_ref, m_ref, acc_ref, *, tq, tk, lambda_init):
    i = pl.program_id(1)
    n_blocks = ((i + 1) * tq + tk - 1) // tk

    @pl.when(i == 0)
    def _():
        vp_ref[:, :HEAD_DIM] = v_ref[...]
        vp_ref[:, HEAD_DIM:] = jnp.ones((v_ref.shape[0], HEAD_DIM), BF16)

    q = q_ref[...]
    lane_q = lax.broadcasted_iota(jnp.int32, q.shape, 1)
    zero = jnp.zeros_like(q)
    q_st = jnp.concatenate([jnp.where(lane_q < DIFF_QK_DIM, q, zero), jnp.where(lane_q >= DIFF_QK_DIM, q, zero)],
                           axis=0)
    lane = lax.broadcasted_iota(jnp.int32, (2 * tq, LANES), 1)
    row = lax.broadcasted_iota(jnp.int32, (2 * tq, LANES), 0)
    q_pos = i * tq + jnp.where(row >= tq, row - tq, row)

    m_ref[...] = jnp.full_like(m_ref, -jnp.inf)
    acc_ref[...] = jnp.zeros_like(acc_ref)

    def key_start(b):
        return pl.multiple_of(jnp.maximum(n_blocks - 1 - b, 0) * tk, tk)

    def scores(b, slot):
        s_ref[slot] = _qk(q_st, k_ref[pl.ds(key_start(b), tk), :])

    def softmax(slot, diagonal):
        thr = q_pos - (n_blocks - 1) * tk

        def chunk(c):
            s_c = s_ref[slot, :, c * LANES:(c + 1) * LANES]
            return jnp.where(lane + c * LANES <= thr, s_c, -jnp.inf) if diagonal else s_c

        m_blk = chunk(0)
        for c in range(1, tk // LANES):
            m_blk = jnp.maximum(m_blk, chunk(c))
        m_old = m_ref[...]
        m_new = jnp.maximum(m_old, jnp.max(m_blk, axis=-1, keepdims=True))
        alpha_ref[slot] = jnp.exp2(m_old - m_new)
        m_ref[...] = m_new
        for c in range(tk // LANES):
            p_ref[slot, :, c * LANES:(c + 1) * LANES] = jnp.exp2(chunk(c) - m_new).astype(BF16)

    def weighted_sum(b, slot):
        pv = jnp.dot(p_ref[slot], vp_ref[pl.ds(key_start(b), tk), :], preferred_element_type=F32)
        alpha = alpha_ref[slot]
        acc_ref[:, :HEAD_DIM] = alpha * acc_ref[:, :HEAD_DIM] + pv[:, :HEAD_DIM]
        acc_ref[:, HEAD_DIM:] = alpha * acc_ref[:, HEAD_DIM:] + pv[:, HEAD_DIM:]

    scores(0, 0)
    scores(1, 1)
    softmax(0, True)

    def body(j, carry):
        b = 2 * j
        weighted_sum(b, 0)
        softmax(1, False)
        scores(b + 2, 0)
        weighted_sum(b + 1, 1)
        softmax(0, False)
        scores(b + 3, 1)
        return carry

    n_pairs = (n_blocks - 1) // 2
    lax.fori_loop(0, n_pairs, body, 0)
    last_is_odd = n_blocks - 2 * n_pairs == 2

    @pl.when(last_is_odd)
    def _():
        softmax(1, False)

    weighted_sum(2 * n_pairs, 0)

    @pl.when(last_is_odd)
    def _():
        weighted_sum(2 * n_pairs + 1, 1)

    lam = (jnp.exp(jnp.sum(lq1_ref[...] * lk1_ref[...], keepdims=True))
           - jnp.exp(jnp.sum(lq2_ref[...] * lk2_ref[...], keepdims=True)) + lambda_init)
    acc = acc_ref[...]
    o1 = acc[:tq, :HEAD_DIM] / acc[:tq, HEAD_DIM:]
    o2 = acc[tq:, :HEAD_DIM] / acc[tq:, HEAD_DIM:]
    out = o1 - lam * o2
    o_ref[...] = ((_rms(out) * g_ref[...]) * (1.0 - lambda_init)).astype(o_ref.dtype)


def diff_attention(proj, lq1, lk1, lq2, lk2, subln, *, n_heads, q_col, k_col, v_col, lambda_init, tq=512, tk=512):
    s = proj.shape[0]
    tq, tk = min(tq, s), min(tk, s)
    assert tk % tq == 0 and s % tk == 0
    qb, kb, vb = q_col // HEAD_DIM, k_col // HEAD_DIM, v_col // HEAD_DIM
    lam_spec = pl.BlockSpec((1, DIFF_QK_DIM), lambda h, i: (0, 0))
    return pl.pallas_call(
        functools.partial(_diff_attn_kernel, tq=tq, tk=tk, lambda_init=lambda_init),
        grid=(n_heads, s // tq),
        in_specs=[
            pl.BlockSpec((tq, HEAD_DIM), lambda h, i: (i, qb + h)),
            pl.BlockSpec((s, HEAD_DIM), lambda h, i: (0, kb + h)),
            pl.BlockSpec((s, HEAD_DIM), lambda h, i: (0, vb + h)),
            lam_spec, lam_spec, lam_spec, lam_spec,
            pl.BlockSpec((1, HEAD_DIM), lambda h, i: (0, 0)),
        ],
        out_specs=pl.BlockSpec((tq, HEAD_DIM), lambda h, i: (i, h)),
        out_shape=jax.ShapeDtypeStruct((s, n_heads * HEAD_DIM), BF16),
        scratch_shapes=[
            pltpu.VMEM((s, 2 * HEAD_DIM), BF16),
            pltpu.VMEM((2, 2 * tq, tk), F32),
            pltpu.VMEM((2, 2 * tq, tk), BF16),
            pltpu.VMEM((2, 2 * tq, LANES), F32),
            pltpu.VMEM((2 * tq, LANES), F32),
            pltpu.VMEM((2 * tq, 2 * HEAD_DIM), F32),
        ],
        compiler_params=_params("arbitrary", "arbitrary"),
        name="diff_attention",
    )(proj, proj, proj, lq1.reshape(1, -1), lk1.reshape(1, -1), lq2.reshape(1, -1), lk2.reshape(1, -1),
      subln.reshape(1, -1))


def _sb_attn_kernel(q_ref, k_ref, v_ref, tri_ref, g_ref, o_ref, c_ref, acc_ref, *, tq, tk, scale):
    i = pl.program_id(1)
    q = q_ref[...]
    tri = tri_ref[...]
    c_ref[...] = jnp.zeros_like(c_ref)
    acc_ref[...] = jnp.zeros_like(acc_ref)

    def block(kb, masked):
        start = pl.multiple_of(kb * tk, tk)
        k = k_ref[pl.ds(start, tk), :]
        v = v_ref[pl.ds(start, tk), :]
        z = _qk(q, k) * scale
        lsm = -(jnp.maximum(z, 0.0) + jnp.log(1.0 + jnp.exp(-jnp.abs(z))))
        if masked:
            rows = lax.broadcasted_iota(jnp.int32, z.shape, 0)
            cols = lax.broadcasted_iota(jnp.int32, z.shape, 1)
            strict = cols < rows
            lsm = jnp.where(strict, lsm, 0.0)
        hi = lsm.astype(BF16)
        lo = (lsm - hi.astype(F32)).astype(BF16)
        r = jnp.dot(hi, tri, preferred_element_type=F32) + jnp.dot(lo, tri, preferred_element_type=F32)
        c = c_ref[...]
        a = jnp.exp(z + r + c)
        if masked:
            a = jnp.where(strict, a, 0.0)
        acc_ref[...] += jnp.dot(a.astype(BF16), v, preferred_element_type=F32)
        c_new = c + r[:, 0:1]
        c_ref[...] = c_new
        return jnp.max(c_new)

    cmax = block(i, True)

    def cond(carry):
        kb, cm = carry
        return jnp.logical_and(kb >= 0, cm > -SB_EXIT)

    def body(carry):
        kb, _ = carry
        return kb - 1, block(kb, False)

    lax.while_loop(cond, body, (i - 1, cmax))
    o_ref[...] = (_rms(acc_ref[...]) * g_ref[...]).astype(o_ref.dtype)


def sb_attention(proj, gain, *, n_heads, q_col, k_col, v_col, tq=256):
    s = proj.shape[0]
    tq = min(tq, s)
    tk = tq
    qb, kb, vb = q_col // HEAD_DIM, k_col // HEAD_DIM, v_col // HEAD_DIM
    tri = (lax.broadcasted_iota(jnp.int32, (tk, tk), 0) >= lax.broadcasted_iota(jnp.int32, (tk, tk), 1)).astype(BF16)
    return pl.pallas_call(
        functools.partial(_sb_attn_kernel, tq=tq, tk=tk, scale=HEAD_DIM ** -0.5),
        grid=(n_heads, s // tq),
        in_specs=[
            pl.BlockSpec((tq, HEAD_DIM), lambda h, i: (i, qb + h)),
            pl.BlockSpec((s, HEAD_DIM), lambda h, i: (0, kb + h)),
            pl.BlockSpec((s, HEAD_DIM), lambda h, i: (0, vb + h)),
            pl.BlockSpec((tk, tk), lambda h, i: (0, 0)),
            pl.BlockSpec((1, HEAD_DIM), lambda h, i: (0, 0)),
        ],
        out_specs=pl.BlockSpec((tq, HEAD_DIM), lambda h, i: (i, h)),
        out_shape=jax.ShapeDtypeStruct((s, n_heads * HEAD_DIM), BF16),
        scratch_shapes=[pltpu.VMEM((tq, 1), F32), pltpu.VMEM((tq, HEAD_DIM), F32)],
        compiler_params=_params("parallel", "arbitrary"),
        name="sb_attention",
    )(proj, proj, proj, tri, gain.reshape(1, -1))


def _mem_kv_kernel(mem_ref, g_ref, w_ref, o_ref):
    mn = (_rms(mem_ref[...]) * g_ref[...]).astype(BF16)
    o_ref[...] = jnp.dot(mn, w_ref[...], preferred_element_type=F32).astype(o_ref.dtype)


def mem_kv(mem, gain, w_kv):
    mt, d = mem.shape
    n = w_kv.shape[1]
    return pl.pallas_call(
        _mem_kv_kernel,
        grid=(1,),
        in_specs=[pl.BlockSpec((mt, d), lambda i: (0, 0)), pl.BlockSpec((1, d), lambda i: (0, 0)),
                  pl.BlockSpec((d, n), lambda i: (0, 0))],
        out_specs=pl.BlockSpec((mt, n), lambda i: (0, 0)),
        out_shape=jax.ShapeDtypeStruct((mt, n), BF16),
        compiler_params=_params("arbitrary"),
        name="mem_kv",
    )(mem, gain.reshape(1, d), w_kv)


def _xattn_kernel(h_ref, gpre_ref, wq_ref, kv_ref, wo_ref, gpost_ref, gnext_ref, hout_ref, hn_ref, *, scale):
    h = h_ref[...]
    hn = (_rms(h) * gpre_ref[...]).astype(BF16)
    q = jnp.dot(hn, wq_ref[...], preferred_element_type=F32).astype(BF16)
    width = XATTN_HEADS * HEAD_DIM
    outs = []
    for hd in range(XATTN_HEADS):
        cols = slice(hd * HEAD_DIM, (hd + 1) * HEAD_DIM)
        k = kv_ref[:, cols]
        v = kv_ref[:, width + hd * HEAD_DIM:width + (hd + 1) * HEAD_DIM]
        sc = _qk(q[:, cols], k) * scale
        sc = sc - jnp.max(sc, axis=-1, keepdims=True)
        e = jnp.exp(sc)
        p = e / jnp.sum(e, axis=-1, keepdims=True)
        outs.append(jnp.dot(p.astype(BF16), v, preferred_element_type=F32).astype(BF16))
    o = jnp.concatenate(outs, axis=-1)
    y = jnp.dot(o, wo_ref[...], preferred_element_type=F32)
    h2 = h + _rms(y) * gpost_ref[...]
    hout_ref[...] = h2
    hn_ref[...] = (_rms(h2) * gnext_ref[...]).astype(hn_ref.dtype)


def xattn_block(h, gpre, wq, kv, wo, gpost, gnext, *, tm=256):
    m, d = h.shape
    tm = min(tm, m)
    width = wq.shape[1]
    row = pl.BlockSpec((tm, d), lambda i: (i, 0))
    vec = pl.BlockSpec((1, d), lambda i: (0, 0))
    return pl.pallas_call(
        functools.partial(_xattn_kernel, scale=HEAD_DIM ** -0.5),
        grid=(m // tm,),
        in_specs=[row, vec, pl.BlockSpec((d, width), lambda i: (0, 0)),
                  pl.BlockSpec(kv.shape, lambda i: (0, 0)), pl.BlockSpec((width, d), lambda i: (0, 0)), vec, vec],
        out_specs=[row, row],
        out_shape=[jax.ShapeDtypeStruct((m, d), F32), jax.ShapeDtypeStruct((m, d), BF16)],
        compiler_params=_params("parallel"),
        name="xattn_block",
    )(h, gpre.reshape(1, d), wq, kv, wo, gpost.reshape(1, d), gnext.reshape(1, d))


def kernel(x, mem, positions, ffn1_norm_pre, ffn1_norm_post, ffn1_w_gate, ffn1_w_up, ffn1_w_down, mix_norm_pre, mix_norm_post, w_in, w_out, lambda_q1, lambda_k1, lambda_q2, lambda_k2, diff_subln, sb_norm, xattn_norm_pre, xattn_norm_post, mem_norm, xattn_w_q, xattn_w_kv, xattn_w_o, ffn2_norm_pre, ffn2_norm_post, ffn2_w_gate, ffn2_w_up, ffn2_w_down):
    b, s, d = x.shape
    depth = w_in.shape[0]
    in_width = w_in.shape[2]
    group = in_width // 6
    n_heads = group // HEAD_DIM
    outs = []
    for bi in range(b):
        h = x[bi]
        pos_col = positions[bi].astype(F32).reshape(s, 1)
        ct, s1, s2 = rope_tables(pos_col)
        hn = prenorm(h, ffn1_norm_pre[0])
        for l in range(depth):
            lambda_init = 0.8 - 0.6 * math.exp(-0.3 * l)
            cast = lambda w: w[l].astype(BF16)
            act = ffn_up(hn, cast(ffn1_w_gate), cast(ffn1_w_up))
            y = matmul_kblocked(act, cast(ffn1_w_down))
            h, hn = post_pre(h, y, ffn1_norm_post[l], mix_norm_pre[l], 0.5)
            proj = in_proj(hn, cast(w_in), ct, s1, s2, rope_cols=2 * group, scaled_cols=group,
                           q_scale=DIFF_QK_DIM ** -0.5 * math.log2(math.e))
            diff_out = diff_attention(proj, lambda_q1[l], lambda_k1[l], lambda_q2[l], lambda_k2[l], diff_subln[l],
                                      n_heads=n_heads, q_col=0, k_col=group, v_col=2 * group,
                                      lambda_init=lambda_init)
            sb_out = sb_attention(proj, sb_norm[l], n_heads=n_heads, q_col=3 * group, k_col=4 * group,
                                  v_col=5 * group)
            y = out_proj(diff_out, sb_out, cast(w_out))
            h = post_pre(h, y, mix_norm_post[l], None, 1.0)
            kv = mem_kv(mem[bi], mem_norm[l], cast(xattn_w_kv))
            h, hn = xattn_block(h, xattn_norm_pre[l], cast(xattn_w_q), kv, cast(xattn_w_o), xattn_norm_post[l],
                                ffn2_norm_pre[l])
            act = ffn_up(hn, cast(ffn2_w_gate), cast(ffn2_w_up))
            y = matmul_kblocked(act, cast(ffn2_w_down))
            if l + 1 < depth:
                h, hn = post_pre(h, y, ffn2_norm_post[l], ffn1_norm_pre[l + 1], 0.5)
            else:
                h = post_pre(h, y, ffn2_norm_post[l], None, 0.5)
        outs.append(h)
    return jnp.stack(outs, axis=0)
```

```python
import functools
import math

import jax
import jax.numpy as jnp
from jax import lax
from jax.experimental import pallas as pl
from jax.experimental.pallas import tpu as pltpu

F32 = jnp.float32
BF16 = jnp.bfloat16

HEAD_DIM = 128
DIFF_QK_DIM = 64
ROT_DIM = 16
ROPE_THETA = 500000.0
NORM_EPS = 1e-6
XATTN_HEADS = 4
LANES = 128

VMEM_LIMIT_BYTES = 56 * 1024 * 1024
SB_EXIT = 100.0
LOG2_E = math.log2(math.e)


def _params(*sem):
    return pltpu.CompilerParams(dimension_semantics=sem, vmem_limit_bytes=VMEM_LIMIT_BYTES)


def _rms(x):
    return x * lax.rsqrt(jnp.mean(x * x, axis=-1, keepdims=True) + NORM_EPS)


def _prenorm_kernel(x_ref, g_ref, o_ref):
    o_ref[...] = (_rms(x_ref[...]) * g_ref[...]).astype(o_ref.dtype)


def prenorm(x, gain, *, tm=256):
    m, d = x.shape
    return pl.pallas_call(
        _prenorm_kernel,
        grid=(m // tm,),
        in_specs=[pl.BlockSpec((tm, d), lambda i: (i, 0)), pl.BlockSpec((1, d), lambda i: (0, 0))],
        out_specs=pl.BlockSpec((tm, d), lambda i: (i, 0)),
        out_shape=jax.ShapeDtypeStruct((m, d), BF16),
        compiler_params=_params("parallel"),
        name="prenorm",
    )(x, gain.reshape(1, d))


def _post_pre_kernel(h_ref, y_ref, gpost_ref, gpre_ref, hout_ref, hn_ref, *, coef):
    h = h_ref[...] + coef * (_rms(y_ref[...]) * gpost_ref[...])
    hout_ref[...] = h
    hn_ref[...] = (_rms(h) * gpre_ref[...]).astype(hn_ref.dtype)


def _post_kernel(h_ref, y_ref, gpost_ref, hout_ref, *, coef):
    hout_ref[...] = h_ref[...] + coef * (_rms(y_ref[...]) * gpost_ref[...])


def post_pre(h, y, gpost, gpre, coef, *, tm=256):
    m, d = h.shape
    row = pl.BlockSpec((tm, d), lambda i: (i, 0))
    vec = pl.BlockSpec((1, d), lambda i: (0, 0))
    if gpre is None:
        return pl.pallas_call(
            functools.partial(_post_kernel, coef=coef),
            grid=(m // tm,),
            in_specs=[row, row, vec],
            out_specs=row,
            out_shape=jax.ShapeDtypeStruct((m, d), F32),
            compiler_params=_params("parallel"),
            name="post_norm",
        )(h, y, gpost.reshape(1, d))
    return pl.pallas_call(
        functools.partial(_post_pre_kernel, coef=coef),
        grid=(m // tm,),
        in_specs=[row, row, vec, vec],
        out_specs=[row, row],
        out_shape=[jax.ShapeDtypeStruct((m, d), F32), jax.ShapeDtypeStruct((m, d), BF16)],
        compiler_params=_params("parallel"),
        name="post_pre_norm",
    )(h, y, gpost.reshape(1, d), gpre.reshape(1, d))


CAST_ROWS = 512


def _cast_weight_once(w_ref, w_bf):
    @pl.when(pl.program_id(1) == 0)
    def _():
        for r in range(0, w_ref.shape[0], CAST_ROWS):
            w_bf[r:r + CAST_ROWS, :] = w_ref[r:r + CAST_ROWS, :].astype(BF16)


def _ffn_up_kernel(x_ref, wg_ref, wu_ref, o_ref, wg_bf, wu_bf):
    _cast_weight_once(wg_ref, wg_bf)
    _cast_weight_once(wu_ref, wu_bf)
    x = x_ref[...]
    g = jnp.dot(x, wg_bf[...], preferred_element_type=F32)
    u = jnp.dot(x, wu_bf[...], preferred_element_type=F32)
    o_ref[...] = (g * jax.nn.sigmoid(g) * u).astype(o_ref.dtype)


def ffn_up(xn, wg, wu, *, tm=512, tn=512):
    m, k = xn.shape
    n = wg.shape[1]
    tm, tn = min(tm, m), min(tn, n)
    assert k % CAST_ROWS == 0
    return pl.pallas_call(
        _ffn_up_kernel,
        grid=(n // tn, m // tm),
        in_specs=[
            pl.BlockSpec((tm, k), lambda j, i: (i, 0)),
            pl.BlockSpec((k, tn), lambda j, i: (0, j)),
            pl.BlockSpec((k, tn), lambda j, i: (0, j)),
        ],
        out_specs=pl.BlockSpec((tm, tn), lambda j, i: (i, j)),
        out_shape=jax.ShapeDtypeStruct((m, n), BF16),
        scratch_shapes=[pltpu.VMEM((k, tn), BF16), pltpu.VMEM((k, tn), BF16)],
        compiler_params=_params("arbitrary", "arbitrary"),
        name="ffn_up",
    )(xn, wg, wu)


def _mm_acc_kernel(a_ref, b_ref, o_ref):
    @pl.when(pl.program_id(2) == 0)
    def _():
        o_ref[...] = jnp.zeros_like(o_ref)

    o_ref[...] += jnp.dot(a_ref[...], b_ref[...].astype(BF16), preferred_element_type=F32)


def matmul_kblocked(a, b, *, tm=2048, tn=1024, tk=1024):
    m, k = a.shape
    n = b.shape[1]
    tm, tn, tk = min(tm, m), min(tn, n), min(tk, k)
    return pl.pallas_call(
        _mm_acc_kernel,
        grid=(m // tm, n // tn, k // tk),
        in_specs=[
            pl.BlockSpec((tm, tk), lambda i, j, l: (i, l)),
            pl.BlockSpec((tk, tn), lambda i, j, l: (l, j)),
        ],
        out_specs=pl.BlockSpec((tm, tn), lambda i, j, l: (i, j)),
        out_shape=jax.ShapeDtypeStruct((m, n), F32),
        compiler_params=_params("parallel", "parallel", "arbitrary"),
        name="matmul_kblocked",
    )(a, b)


def _out_proj_kernel(a_ref, b_ref, wa_ref, wb_ref, o_ref):
    o_ref[...] = jnp.dot(a_ref[...], wa_ref[...], preferred_element_type=F32) + jnp.dot(
        b_ref[...], wb_ref[...], preferred_element_type=F32
    )


def out_proj(a, b, w, *, tm=1024, tn=1024):
    m, ka = a.shape
    kb = b.shape[1]
    n = w.shape[1]
    tm, tn = min(tm, m), min(tn, n)
    assert ka == kb
    return pl.pallas_call(
        _out_proj_kernel,
        grid=(m // tm, n // tn),
        in_specs=[
            pl.BlockSpec((tm, ka), lambda i, j: (i, 0)),
            pl.BlockSpec((tm, kb), lambda i, j: (i, 0)),
            pl.BlockSpec((ka, tn), lambda i, j: (0, j)),
            pl.BlockSpec((kb, tn), lambda i, j: (1, j)),
        ],
        out_specs=pl.BlockSpec((tm, tn), lambda i, j: (i, j)),
        out_shape=jax.ShapeDtypeStruct((m, n), F32),
        compiler_params=_params("parallel", "arbitrary"),
        name="out_proj",
    )(a, b, w, w)


def _rope_table_kernel(pos_ref, invf_ref, ct_ref, s1_ref, s2_ref):
    ang = pos_ref[...] * invf_ref[...]
    cos, sin = jnp.cos(ang), jnp.sin(ang)
    lane = lax.broadcasted_iota(jnp.int32, ang.shape, 1) % DIFF_QK_DIM
    half = ROT_DIM // 2
    ct_ref[...] = jnp.where(lane < ROT_DIM, cos, 1.0)
    s1_ref[...] = jnp.where(lane < half, -sin, 0.0)
    s2_ref[...] = jnp.where((lane >= half) & (lane < ROT_DIM), sin, 0.0)


def rope_tables(positions_col, *, tm=1024):
    s = positions_col.shape[0]
    tm = min(tm, s)
    half = ROT_DIM // 2
    inv_freq = ROPE_THETA ** (-jnp.arange(0, ROT_DIM, 2, dtype=F32) / ROT_DIM)
    invf = jnp.tile(inv_freq, LANES // half).reshape(1, LANES)
    tab = pl.BlockSpec((tm, LANES), lambda i: (i, 0))
    return pl.pallas_call(
        _rope_table_kernel,
        grid=(s // tm,),
        in_specs=[pl.BlockSpec((tm, 1), lambda i: (i, 0)), pl.BlockSpec((1, LANES), lambda i: (0, 0))],
        out_specs=[tab, tab, tab],
        out_shape=[jax.ShapeDtypeStruct((s, LANES), F32)] * 3,
        compiler_params=_params("parallel"),
        name="rope_tables",
    )(positions_col, invf)


def _in_proj_plain_kernel(x_ref, w_ref, o_ref, w_bf, *, scaled_tiles, scale):
    _cast_weight_once(w_ref, w_bf)
    j = pl.program_id(0)
    s = jnp.where((j >= scaled_tiles[0]) & (j < scaled_tiles[1]), scale, 1.0).astype(F32)
    o_ref[...] = (jnp.dot(x_ref[...], w_bf[...], preferred_element_type=F32) * s).astype(o_ref.dtype)


def _in_proj_rope_kernel(x_ref, w_ref, ct_ref, s1_ref, s2_ref, o_ref, w_bf, *, n_scaled_tiles, q_scale):
    _cast_weight_once(w_ref, w_bf)
    acc = jnp.dot(x_ref[...], w_bf[...], preferred_element_type=F32)
    scale = jnp.where(pl.program_id(0) < n_scaled_tiles, q_scale, 1.0).astype(F32)
    ct, s1, s2 = ct_ref[...], s1_ref[...], s2_ref[...]
    half = ROT_DIM // 2
    for c in range(acc.shape[1] // LANES):
        a = acc[:, c * LANES:(c + 1) * LANES]
        r = a * ct + pltpu.roll(a, LANES - half, 1) * s1 + pltpu.roll(a, half, 1) * s2
        o_ref[:, c * LANES:(c + 1) * LANES] = (r * scale).astype(o_ref.dtype)


def in_proj(xn, w, ct, s1, s2, *, rope_cols, scaled_cols, q_scale, sb_q_cols, sb_q_scale, tm=1024, tn=512):
    m, k = xn.shape
    n = w.shape[1]
    tm, tn = min(tm, m), min(tn, n, scaled_cols)
    assert rope_cols % tn == 0 and scaled_cols % tn == 0 and k % CAST_ROWS == 0
    assert all((c - rope_cols) % tn == 0 for c in sb_q_cols)
    sb_tiles = tuple((c - rope_cols) // tn for c in sb_q_cols)
    n_rope_tiles = rope_cols // tn
    x_spec = pl.BlockSpec((tm, k), lambda j, i: (i, 0))
    out_spec = pl.BlockSpec((tm, tn), lambda j, i: (i, j))
    tab = pl.BlockSpec((tm, LANES), lambda j, i: (i, 0))
    scratch = [pltpu.VMEM((k, tn), BF16)]
    roped = pl.pallas_call(
        functools.partial(_in_proj_rope_kernel, n_scaled_tiles=scaled_cols // tn, q_scale=q_scale),
        grid=(n_rope_tiles, m // tm),
        in_specs=[x_spec, pl.BlockSpec((k, tn), lambda j, i: (0, j)), tab, tab, tab],
        out_specs=out_spec,
        out_shape=jax.ShapeDtypeStruct((m, rope_cols), BF16),
        scratch_shapes=scratch,
        compiler_params=_params("arbitrary", "arbitrary"),
        name="in_proj_rope",
    )(xn, w, ct, s1, s2)
    plain = pl.pallas_call(
        functools.partial(_in_proj_plain_kernel, scaled_tiles=sb_tiles, scale=sb_q_scale),
        grid=((n - rope_cols) // tn, m // tm),
        in_specs=[x_spec, pl.BlockSpec((k, tn), lambda j, i: (0, j + n_rope_tiles))],
        out_specs=out_spec,
        out_shape=jax.ShapeDtypeStruct((m, n - rope_cols), BF16),
        scratch_shapes=scratch,
        compiler_params=_params("arbitrary", "arbitrary"),
        name="in_proj_plain",
    )(xn, w)
    return roped, plain


def _qk(q, k):
    return lax.dot_general(q, k, (((1,), (1,)), ((), ())), preferred_element_type=F32)


def _diff_attn_kernel(q_ref, k_ref, v_ref, lq1_ref, lk1_ref, lq2_ref, lk2_ref, g_ref, o_ref,
                      vp_ref, s_ref, p_ref, alpha_ref, m_ref, acc_ref, *, tq, tk, lambda_init):
    i = pl.program_id(1)
    n_blocks = ((i + 1) * tq + tk - 1) // tk

    @pl.when(i == 0)
    def _():
        vp_ref[:, :HEAD_DIM] = v_ref[...]
        vp_ref[:, HEAD_DIM:] = jnp.ones((v_ref.shape[0], HEAD_DIM), BF16)

    q = q_ref[...]
    lane_q = lax.broadcasted_iota(jnp.int32, q.shape, 1)
    zero = jnp.zeros_like(q)
    q_st = jnp.concatenate([jnp.where(lane_q < DIFF_QK_DIM, q, zero), jnp.where(lane_q >= DIFF_QK_DIM, q, zero)],
                           axis=0)
    lane = lax.broadcasted_iota(jnp.int32, (2 * tq, LANES), 1)
    row = lax.broadcasted_iota(jnp.int32, (2 * tq, LANES), 0)
    q_pos = i * tq + jnp.where(row >= tq, row - tq, row)

    m_ref[...] = jnp.full_like(m_ref, -jnp.inf)
    acc_ref[...] = jnp.zeros_like(acc_ref)

    def key_start(b):
        return pl.multiple_of(jnp.maximum(n_blocks - 1 - b, 0) * tk, tk)

    def scores(b, slot):
        s_ref[slot] = _qk(q_st, k_ref[pl.ds(key_start(b), tk), :])

    def softmax(slot, diagonal):
        thr = q_pos - (n_blocks - 1) * tk

        def chunk(c):
            s_c = s_ref[slot, :, c * LANES:(c + 1) * LANES]
            return jnp.where(lane + c * LANES <= thr, s_c, -jnp.inf) if diagonal else s_c

        m_blk = chunk(0)
        for c in range(1, tk // LANES):
            m_blk = jnp.maximum(m_blk, chunk(c))
        m_old = m_ref[...]
        m_new = jnp.maximum(m_old, jnp.max(m_blk, axis=-1, keepdims=True))
        alpha_ref[slot] = jnp.exp2(m_old - m_new)
        m_ref[...] = m_new
        for c in range(tk // LANES):
            p_ref[slot, :, c * LANES:(c + 1) * LANES] = jnp.exp2(chunk(c) - m_new).astype(BF16)

    def weighted_sum(b, slot):
        pv = jnp.dot(p_ref[slot], vp_ref[pl.ds(key_start(b), tk), :], preferred_element_type=F32)
        alpha = alpha_ref[slot]
        acc_ref[:, :HEAD_DIM] = alpha * acc_ref[:, :HEAD_DIM] + pv[:, :HEAD_DIM]
        acc_ref[:, HEAD_DIM:] = alpha * acc_ref[:, HEAD_DIM:] + pv[:, HEAD_DIM:]

    scores(0, 0)
    scores(1, 1)
    softmax(0, True)

    def body(j, carry):
        b = 2 * j
        weighted_sum(b, 0)
        softmax(1, False)
        scores(b + 2, 0)
        weighted_sum(b + 1, 1)
        softmax(0, False)
        scores(b + 3, 1)
        return carry

    n_pairs = (n_blocks - 1) // 2
    lax.fori_loop(0, n_pairs, body, 0)
    last_is_odd = n_blocks - 2 * n_pairs == 2

    @pl.when(last_is_odd)
    def _():
        softmax(1, False)

    weighted_sum(2 * n_pairs, 0)

    @pl.when(last_is_odd)
    def _():
        weighted_sum(2 * n_pairs + 1, 1)

    lam = (jnp.exp(jnp.sum(lq1_ref[...] * lk1_ref[...], keepdims=True))
           - jnp.exp(jnp.sum(lq2_ref[...] * lk2_ref[...], keepdims=True)) + lambda_init)
    acc = acc_ref[...]
    o1 = acc[:tq, :HEAD_DIM] / acc[:tq, HEAD_DIM:]
    o2 = acc[tq:, :HEAD_DIM] / acc[tq:, HEAD_DIM:]
    out = o1 - lam * o2
    o_ref[...] = ((_rms(out) * g_ref[...]) * (1.0 - lambda_init)).astype(o_ref.dtype)


def diff_attention(qkv, lq1, lk1, lq2, lk2, subln, *, n_heads, q_col, k_col, v_col, lambda_init, tq=512, tk=512):
    s = qkv[0].shape[0]
    tq, tk = min(tq, s), min(tk, s)
    assert tk % tq == 0 and s % tk == 0
    qb, kb, vb = q_col // HEAD_DIM, k_col // HEAD_DIM, v_col // HEAD_DIM
    lam_spec = pl.BlockSpec((1, DIFF_QK_DIM), lambda h, i: (0, 0))
    return pl.pallas_call(
        functools.partial(_diff_attn_kernel, tq=tq, tk=tk, lambda_init=lambda_init),
        grid=(n_heads, s // tq),
        in_specs=[
            pl.BlockSpec((tq, HEAD_DIM), lambda h, i: (i, qb + h)),
            pl.BlockSpec((s, HEAD_DIM), lambda h, i: (0, kb + h)),
            pl.BlockSpec((s, HEAD_DIM), lambda h, i: (0, vb + h)),
            lam_spec, lam_spec, lam_spec, lam_spec,
            pl.BlockSpec((1, HEAD_DIM), lambda h, i: (0, 0)),
        ],
        out_specs=pl.BlockSpec((tq, HEAD_DIM), lambda h, i: (i, h)),
        out_shape=jax.ShapeDtypeStruct((s, n_heads * HEAD_DIM), BF16),
        scratch_shapes=[
            pltpu.VMEM((s, 2 * HEAD_DIM), BF16),
            pltpu.VMEM((2, 2 * tq, tk), F32),
            pltpu.VMEM((2, 2 * tq, tk), BF16),
            pltpu.VMEM((2, 2 * tq, LANES), F32),
            pltpu.VMEM((2 * tq, LANES), F32),
            pltpu.VMEM((2 * tq, 2 * HEAD_DIM), F32),
        ],
        compiler_params=_params("arbitrary", "arbitrary"),
        name="diff_attention",
    )(*qkv, lq1.reshape(1, -1), lk1.reshape(1, -1), lq2.reshape(1, -1), lk2.reshape(1, -1), subln.reshape(1, -1))


def _sb_attn_kernel(q_ref, k_ref, v_ref, tri_ref, g_ref, o_ref, c_ref, acc_ref, *, tq, tk):
    i = pl.program_id(1)
    q = q_ref[...]
    tri = tri_ref[...]

    def block(kb, masked, c):
        start = pl.multiple_of(kb * tk, tk)
        k = k_ref[pl.ds(start, tk), :]
        v = v_ref[pl.ds(start, tk), :]
        z = _qk(q, k)
        nz = -z
        lsm = jnp.minimum(nz, 0.0) - jnp.log2(1.0 + jnp.exp2(jnp.minimum(z, nz)))
        if masked:
            q_pos = i * tq + lax.broadcasted_iota(jnp.int32, z.shape, 0)
            k_pos = kb * tk + lax.broadcasted_iota(jnp.int32, z.shape, 1)
            strict = k_pos < q_pos
            lsm = jnp.where(strict, lsm, 0.0)
        hi = lsm.astype(BF16)
        lo = (lsm - hi.astype(F32)).astype(BF16)
        r = jnp.dot(hi, tri, preferred_element_type=F32) + jnp.dot(lo, tri, preferred_element_type=F32)
        a = jnp.exp2(z + r + c)
        if masked:
            a = jnp.where(strict, a, 0.0)
        return jnp.dot(a.astype(BF16), v, preferred_element_type=F32), c + r[:, 0:1]

    n_diag = tq // tk
    kb_top = (i + 1) * n_diag - 1

    def head_blocks(with_left):
        c = jnp.zeros((tq, 1), F32)
        acc = None
        for d in range(n_diag + int(with_left)):
            pv, c = block(kb_top - d, d < n_diag, c)
            acc = pv if acc is None else acc + pv
        acc_ref[...] = acc
        c_ref[...] = c

    pl.when(i == 0)(functools.partial(head_blocks, False))
    pl.when(i > 0)(functools.partial(head_blocks, True))

    def cond(carry):
        kb, cm = carry
        return jnp.logical_and(kb >= 0, cm > -SB_EXIT * LOG2_E)

    def body(carry):
        kb, _ = carry
        pv, c = block(kb, False, c_ref[...])
        acc_ref[...] += pv
        c_ref[...] = c
        return kb - 1, jnp.max(c)

    lax.while_loop(cond, body, (kb_top - n_diag - 1, jnp.max(c_ref[...])))
    o_ref[...] = (_rms(acc_ref[...]) * g_ref[...]).astype(o_ref.dtype)


def sb_attention(qkv, gain, *, n_heads, q_col, k_col, v_col, tq=256, tk=256):
    s = qkv[0].shape[0]
    tq, tk = min(tq, s), min(tk, s)
    assert tq % tk == 0
    qb, kb, vb = q_col // HEAD_DIM, k_col // HEAD_DIM, v_col // HEAD_DIM
    tri = (lax.broadcasted_iota(jnp.int32, (tk, tk), 0) >= lax.broadcasted_iota(jnp.int32, (tk, tk), 1)).astype(BF16)
    return pl.pallas_call(
        functools.partial(_sb_attn_kernel, tq=tq, tk=tk),
        grid=(n_heads, s // tq),
        in_specs=[
            pl.BlockSpec((tq, HEAD_DIM), lambda h, i: (i, qb + h)),
            pl.BlockSpec((s, HEAD_DIM), lambda h, i: (0, kb + h)),
            pl.BlockSpec((s, HEAD_DIM), lambda h, i: (0, vb + h)),
            pl.BlockSpec((tk, tk), lambda h, i: (0, 0)),
            pl.BlockSpec((1, HEAD_DIM), lambda h, i: (0, 0)),
        ],
        out_specs=pl.BlockSpec((tq, HEAD_DIM), lambda h, i: (i, h)),
        out_shape=jax.ShapeDtypeStruct((s, n_heads * HEAD_DIM), BF16),
        scratch_shapes=[pltpu.VMEM((tq, 1), F32), pltpu.VMEM((tq, HEAD_DIM), F32)],
        compiler_params=_params("parallel", "arbitrary"),
        name="sb_attention",
    )(*qkv, tri, gain.reshape(1, -1))


def _mem_kv_kernel(mem_ref, g_ref, w_ref, o_ref):
    mn = (_rms(mem_ref[...]) * g_ref[...]).astype(BF16)
    o_ref[...] = jnp.dot(mn, w_ref[...], preferred_element_type=F32).astype(o_ref.dtype)


def mem_kv(mem, gain, w_kv):
    mt, d = mem.shape
    n = w_kv.shape[1]
    return pl.pallas_call(
        _mem_kv_kernel,
        grid=(1,),
        in_specs=[pl.BlockSpec((mt, d), lambda i: (0, 0)), pl.BlockSpec((1, d), lambda i: (0, 0)),
                  pl.BlockSpec((d, n), lambda i: (0, 0))],
        out_specs=pl.BlockSpec((mt, n), lambda i: (0, 0)),
        out_shape=jax.ShapeDtypeStruct((mt, n), BF16),
        compiler_params=_params("arbitrary"),
        name="mem_kv",
    )(mem, gain.reshape(1, d), w_kv)


def _xattn_kernel(h_ref, gpre_ref, wq_ref, kv_ref, wo_ref, gpost_ref, gnext_ref, hout_ref, hn_ref, *, scale):
    h = h_ref[...]
    hn = (_rms(h) * gpre_ref[...]).astype(BF16)
    q = jnp.dot(hn, wq_ref[...], preferred_element_type=F32).astype(BF16)
    width = XATTN_HEADS * HEAD_DIM
    outs = []
    for hd in range(XATTN_HEADS):
        cols = slice(hd * HEAD_DIM, (hd + 1) * HEAD_DIM)
        k = kv_ref[:, cols]
        v = kv_ref[:, width + hd * HEAD_DIM:width + (hd + 1) * HEAD_DIM]
        sc = _qk(q[:, cols], k) * scale
        sc = sc - jnp.max(sc, axis=-1, keepdims=True)
        e = jnp.exp(sc)
        p = e / jnp.sum(e, axis=-1, keepdims=True)
        outs.append(jnp.dot(p.astype(BF16), v, preferred_element_type=F32).astype(BF16))
    o = jnp.concatenate(outs, axis=-1)
    y = jnp.dot(o, wo_ref[...], preferred_element_type=F32)
    h2 = h + _rms(y) * gpost_ref[...]
    hout_ref[...] = h2
    hn_ref[...] = (_rms(h2) * gnext_ref[...]).astype(hn_ref.dtype)


def xattn_block(h, gpre, wq, kv, wo, gpost, gnext, *, tm=256):
    m, d = h.shape
    tm = min(tm, m)
    width = wq.shape[1]
    row = pl.BlockSpec((tm, d), lambda i: (i, 0))
    vec = pl.BlockSpec((1, d), lambda i: (0, 0))
    return pl.pallas_call(
        functools.partial(_xattn_kernel, scale=HEAD_DIM ** -0.5),
        grid=(m // tm,),
        in_specs=[row, vec, pl.BlockSpec((d, width), lambda i: (0, 0)),
                  pl.BlockSpec(kv.shape, lambda i: (0, 0)), pl.BlockSpec((width, d), lambda i: (0, 0)), vec, vec],
        out_specs=[row, row],
        out_shape=[jax.ShapeDtypeStruct((m, d), F32), jax.ShapeDtypeStruct((m, d), BF16)],
        compiler_params=_params("parallel"),
        name="xattn_block",
    )(h, gpre.reshape(1, d), wq, kv, wo, gpost.reshape(1, d), gnext.reshape(1, d))


def kernel(x, mem, positions, ffn1_norm_pre, ffn1_norm_post, ffn1_w_gate, ffn1_w_up, ffn1_w_down, mix_norm_pre, mix_norm_post, w_in, w_out, lambda_q1, lambda_k1, lambda_q2, lambda_k2, diff_subln, sb_norm, xattn_norm_pre, xattn_norm_post, mem_norm, xattn_w_q, xattn_w_kv, xattn_w_o, ffn2_norm_pre, ffn2_norm_post, ffn2_w_gate, ffn2_w_up, ffn2_w_down):
    b, s, d = x.shape
    depth = w_in.shape[0]
    in_width = w_in.shape[2]
    group = in_width // 6
    n_heads = group // HEAD_DIM
    outs = []
    for bi in range(b):
        h = x[bi]
        pos_col = positions[bi].astype(F32).reshape(s, 1)
        ct, s1, s2 = rope_tables(pos_col)
        hn = prenorm(h, ffn1_norm_pre[0])
        for l in range(depth):
            lambda_init = 0.8 - 0.6 * math.exp(-0.3 * l)
            cast = lambda w: w[l].astype(BF16)
            act = ffn_up(hn, ffn1_w_gate[l], ffn1_w_up[l])
            y = matmul_kblocked(act, ffn1_w_down[l])
            h, hn = post_pre(h, y, ffn1_norm_post[l], mix_norm_pre[l], 0.5)
            qk, rest = in_proj(hn, w_in[l], ct, s1, s2, rope_cols=2 * group, scaled_cols=group,
                               q_scale=DIFF_QK_DIM ** -0.5 * LOG2_E, sb_q_cols=(3 * group, 4 * group),
                               sb_q_scale=HEAD_DIM ** -0.5 * LOG2_E)
            diff_out = diff_attention((qk, qk, rest), lambda_q1[l], lambda_k1[l], lambda_q2[l], lambda_k2[l],
                                      diff_subln[l], n_heads=n_heads, q_col=0, k_col=group, v_col=0,
                                      lambda_init=lambda_init)
            sb_out = sb_attention((rest, rest, rest), sb_norm[l], n_heads=n_heads, q_col=group, k_col=2 * group,
                                  v_col=3 * group)
            y = out_proj(diff_out, sb_out, cast(w_out))
            h = post_pre(h, y, mix_norm_post[l], None, 1.0)
            kv = mem_kv(mem[bi], mem_norm[l], cast(xattn_w_kv))
            h, hn = xattn_block(h, xattn_norm_pre[l], cast(xattn_w_q), kv, cast(xattn_w_o), xattn_norm_post[l],
                                ffn2_norm_pre[l])
            act = ffn_up(hn, ffn2_w_gate[l], ffn2_w_up[l])
            y = matmul_kblocked(act, ffn2_w_down[l])
            if l + 1 < depth:
                h, hn = post_pre(h, y, ffn2_norm_post[l], ffn1_norm_pre[l + 1], 0.5)
            else:
                h = post_pre(h, y, ffn2_norm_post[l], None, 0.5)
        outs.append(h)
    return jnp.stack(outs, axis=0)
```

```python
import functools
import math

import jax
import jax.numpy as jnp
from jax import lax
from jax.experimental import pallas as pl
from jax.experimental.pallas import tpu as pltpu

F32 = jnp.float32
BF16 = jnp.bfloat16

HEAD_DIM = 128
DIFF_QK_DIM = 64
ROT_DIM = 16
ROPE_THETA = 500000.0
NORM_EPS = 1e-6
XATTN_HEADS = 4
LANES = 128

VMEM_LIMIT_BYTES = 56 * 1024 * 1024
SB_EXIT = 100.0
LOG2_E = math.log2(math.e)


def _params(*sem):
    return pltpu.CompilerParams(dimension_semantics=sem, vmem_limit_bytes=VMEM_LIMIT_BYTES)


def _rms(x):
    return x * lax.rsqrt(jnp.mean(x * x, axis=-1, keepdims=True) + NORM_EPS)


def _prenorm_kernel(x_ref, g_ref, o_ref):
    o_ref[...] = (_rms(x_ref[...]) * g_ref[...]).astype(o_ref.dtype)


def prenorm(x, gain, *, tm=256):
    m, d = x.shape
    return pl.pallas_call(
        _prenorm_kernel,
        grid=(m // tm,),
        in_specs=[pl.BlockSpec((tm, d), lambda i: (i, 0)), pl.BlockSpec((1, d), lambda i: (0, 0))],
        out_specs=pl.BlockSpec((tm, d), lambda i: (i, 0)),
        out_shape=jax.ShapeDtypeStruct((m, d), BF16),
        compiler_params=_params("parallel"),
        name="prenorm",
    )(x, gain.reshape(1, d))


def _post_pre_kernel(h_ref, y_ref, gpost_ref, gpre_ref, hout_ref, hn_ref, *, coef):
    h = h_ref[...] + coef * (_rms(y_ref[...]) * gpost_ref[...])
    hout_ref[...] = h
    hn_ref[...] = (_rms(h) * gpre_ref[...]).astype(hn_ref.dtype)


def _post_kernel(h_ref, y_ref, gpost_ref, hout_ref, *, coef):
    hout_ref[...] = h_ref[...] + coef * (_rms(y_ref[...]) * gpost_ref[...])


def post_pre(h, y, gpost, gpre, coef, *, tm=256):
    m, d = h.shape
    row = pl.BlockSpec((tm, d), lambda i: (i, 0))
    vec = pl.BlockSpec((1, d), lambda i: (0, 0))
    if gpre is None:
        return pl.pallas_call(
            functools.partial(_post_kernel, coef=coef),
            grid=(m // tm,),
            in_specs=[row, row, vec],
            out_specs=row,
            out_shape=jax.ShapeDtypeStruct((m, d), F32),
            compiler_params=_params("parallel"),
            name="post_norm",
        )(h, y, gpost.reshape(1, d))
    return pl.pallas_call(
        functools.partial(_post_pre_kernel, coef=coef),
        grid=(m // tm,),
        in_specs=[row, row, vec, vec],
        out_specs=[row, row],
        out_shape=[jax.ShapeDtypeStruct((m, d), F32), jax.ShapeDtypeStruct((m, d), BF16)],
        compiler_params=_params("parallel"),
        name="post_pre_norm",
    )(h, y, gpost.reshape(1, d), gpre.reshape(1, d))


CAST_ROWS = 512


def _cast_weight_once(w_ref, w_bf):
    @pl.when(pl.program_id(1) == 0)
    def _():
        for r in range(0, w_ref.shape[0], CAST_ROWS):
            w_bf[r:r + CAST_ROWS, :] = w_ref[r:r + CAST_ROWS, :].astype(BF16)


def _ffn_up_kernel(x_ref, wg_ref, wu_ref, o_ref, wg_bf, wu_bf):
    _cast_weight_once(wg_ref, wg_bf)
    _cast_weight_once(wu_ref, wu_bf)
    x = x_ref[...]
    g = jnp.dot(x, wg_bf[...], preferred_element_type=F32)
    u = jnp.dot(x, wu_bf[...], preferred_element_type=F32)
    o_ref[...] = (g * jax.nn.sigmoid(g) * u).astype(o_ref.dtype)


def ffn_up(xn, wg, wu, *, tm=512, tn=512):
    m, k = xn.shape
    n = wg.shape[1]
    tm, tn = min(tm, m), min(tn, n)
    assert k % CAST_ROWS == 0
    return pl.pallas_call(
        _ffn_up_kernel,
        grid=(n // tn, m // tm),
        in_specs=[
            pl.BlockSpec((tm, k), lambda j, i: (i, 0)),
            pl.BlockSpec((k, tn), lambda j, i: (0, j)),
            pl.BlockSpec((k, tn), lambda j, i: (0, j)),
        ],
        out_specs=pl.BlockSpec((tm, tn), lambda j, i: (i, j)),
        out_shape=jax.ShapeDtypeStruct((m, n), BF16),
        scratch_shapes=[pltpu.VMEM((k, tn), BF16), pltpu.VMEM((k, tn), BF16)],
        compiler_params=_params("arbitrary", "arbitrary"),
        name="ffn_up",
    )(xn, wg, wu)


def _mm_acc_kernel(a_ref, b_ref, o_ref):
    @pl.when(pl.program_id(2) == 0)
    def _():
        o_ref[...] = jnp.zeros_like(o_ref)

    o_ref[...] += jnp.dot(a_ref[...], b_ref[...].astype(BF16), preferred_element_type=F32)


def matmul_kblocked(a, b, *, tm=2048, tn=1024, tk=1024):
    m, k = a.shape
    n = b.shape[1]
    tm, tn, tk = min(tm, m), min(tn, n), min(tk, k)
    return pl.pallas_call(
        _mm_acc_kernel,
        grid=(m // tm, n // tn, k // tk),
        in_specs=[
            pl.BlockSpec((tm, tk), lambda i, j, l: (i, l)),
            pl.BlockSpec((tk, tn), lambda i, j, l: (l, j)),
        ],
        out_specs=pl.BlockSpec((tm, tn), lambda i, j, l: (i, j)),
        out_shape=jax.ShapeDtypeStruct((m, n), F32),
        compiler_params=_params("parallel", "parallel", "arbitrary"),
        name="matmul_kblocked",
    )(a, b)


def _out_proj_kernel(a_ref, b_ref, wa_ref, wb_ref, o_ref):
    o_ref[...] = jnp.dot(a_ref[...], wa_ref[...], preferred_element_type=F32) + jnp.dot(
        b_ref[...], wb_ref[...], preferred_element_type=F32
    )


def out_proj(a, b, w, *, tm=1024, tn=1024):
    m, ka = a.shape
    kb = b.shape[1]
    n = w.shape[1]
    tm, tn = min(tm, m), min(tn, n)
    assert ka == kb
    return pl.pallas_call(
        _out_proj_kernel,
        grid=(m // tm, n // tn),
        in_specs=[
            pl.BlockSpec((tm, ka), lambda i, j: (i, 0)),
            pl.BlockSpec((tm, kb), lambda i, j: (i, 0)),
            pl.BlockSpec((ka, tn), lambda i, j: (0, j)),
            pl.BlockSpec((kb, tn), lambda i, j: (1, j)),
        ],
        out_specs=pl.BlockSpec((tm, tn), lambda i, j: (i, j)),
        out_shape=jax.ShapeDtypeStruct((m, n), F32),
        compiler_params=_params("parallel", "arbitrary"),
        name="out_proj",
    )(a, b, w, w)


def _rope_table_kernel(pos_ref, invf_ref, ct_ref, s1_ref, s2_ref):
    ang = pos_ref[...] * invf_ref[...]
    cos, sin = jnp.cos(ang), jnp.sin(ang)
    lane = lax.broadcasted_iota(jnp.int32, ang.shape, 1) % DIFF_QK_DIM
    half = ROT_DIM // 2
    ct_ref[...] = jnp.where(lane < ROT_DIM, cos, 1.0)
    s1_ref[...] = jnp.where(lane < half, -sin, 0.0)
    s2_ref[...] = jnp.where((lane >= half) & (lane < ROT_DIM), sin, 0.0)


def rope_tables(positions_col, *, tm=1024):
    s = positions_col.shape[0]
    tm = min(tm, s)
    half = ROT_DIM // 2
    inv_freq = ROPE_THETA ** (-jnp.arange(0, ROT_DIM, 2, dtype=F32) / ROT_DIM)
    invf = jnp.tile(inv_freq, LANES // half).reshape(1, LANES)
    tab = pl.BlockSpec((tm, LANES), lambda i: (i, 0))
    return pl.pallas_call(
        _rope_table_kernel,
        grid=(s // tm,),
        in_specs=[pl.BlockSpec((tm, 1), lambda i: (i, 0)), pl.BlockSpec((1, LANES), lambda i: (0, 0))],
        out_specs=[tab, tab, tab],
        out_shape=[jax.ShapeDtypeStruct((s, LANES), F32)] * 3,
        compiler_params=_params("parallel"),
        name="rope_tables",
    )(positions_col, invf)


def _in_proj_plain_kernel(x_ref, w_ref, o_ref, w_bf, *, scaled_tiles, scale):
    _cast_weight_once(w_ref, w_bf)
    j = pl.program_id(0)
    s = jnp.where((j >= scaled_tiles[0]) & (j < scaled_tiles[1]), scale, 1.0).astype(F32)
    o_ref[...] = (jnp.dot(x_ref[...], w_bf[...], preferred_element_type=F32) * s).astype(o_ref.dtype)


def _in_proj_rope_kernel(x_ref, w_ref, ct_ref, s1_ref, s2_ref, o_ref, w_bf, *, n_scaled_tiles, q_scale):
    _cast_weight_once(w_ref, w_bf)
    acc = jnp.dot(x_ref[...], w_bf[...], preferred_element_type=F32)
    scale = jnp.where(pl.program_id(0) < n_scaled_tiles, q_scale, 1.0).astype(F32)
    ct, s1, s2 = ct_ref[...], s1_ref[...], s2_ref[...]
    half = ROT_DIM // 2
    for c in range(acc.shape[1] // LANES):
        a = acc[:, c * LANES:(c + 1) * LANES]
        r = a * ct + pltpu.roll(a, LANES - half, 1) * s1 + pltpu.roll(a, half, 1) * s2
        o_ref[:, c * LANES:(c + 1) * LANES] = (r * scale).astype(o_ref.dtype)


def in_proj(xn, w, ct, s1, s2, *, rope_cols, scaled_cols, q_scale, sb_q_cols, sb_q_scale, tm=1024, tn=512):
    m, k = xn.shape
    n = w.shape[1]
    tm, tn = min(tm, m), min(tn, n, scaled_cols)
    assert rope_cols % tn == 0 and scaled_cols % tn == 0 and k % CAST_ROWS == 0
    assert all((c - rope_cols) % tn == 0 for c in sb_q_cols)
    sb_tiles = tuple((c - rope_cols) // tn for c in sb_q_cols)
    n_rope_tiles = rope_cols // tn
    x_spec = pl.BlockSpec((tm, k), lambda j, i: (i, 0))
    out_spec = pl.BlockSpec((tm, tn), lambda j, i: (i, j))
    tab = pl.BlockSpec((tm, LANES), lambda j, i: (i, 0))
    scratch = [pltpu.VMEM((k, tn), BF16)]
    roped = pl.pallas_call(
        functools.partial(_in_proj_rope_kernel, n_scaled_tiles=scaled_cols // tn, q_scale=q_scale),
        grid=(n_rope_tiles, m // tm),
        in_specs=[x_spec, pl.BlockSpec((k, tn), lambda j, i: (0, j)), tab, tab, tab],
        out_specs=out_spec,
        out_shape=jax.ShapeDtypeStruct((m, rope_cols), BF16),
        scratch_shapes=scratch,
        compiler_params=_params("arbitrary", "arbitrary"),
        name="in_proj_rope",
    )(xn, w, ct, s1, s2)
    plain = pl.pallas_call(
        functools.partial(_in_proj_plain_kernel, scaled_tiles=sb_tiles, scale=sb_q_scale),
        grid=((n - rope_cols) // tn, m // tm),
        in_specs=[x_spec, pl.BlockSpec((k, tn), lambda j, i: (0, j + n_rope_tiles))],
        out_specs=out_spec,
        out_shape=jax.ShapeDtypeStruct((m, n - rope_cols), BF16),
        scratch_shapes=scratch,
        compiler_params=_params("arbitrary", "arbitrary"),
        name="in_proj_plain",
    )(xn, w)
    return roped, plain


def _qk(q, k):
    return lax.dot_general(q, k, (((1,), (1,)), ((), ())), preferred_element_type=F32)


def _diff_attn_kernel(q_ref, k_ref, v_ref, lq1_ref, lk1_ref, lq2_ref, lk2_ref, g_ref, o_ref,
                      vp_ref, s_ref, p_ref, alpha_ref, m_ref, acc_ref, *, tq, tk, lambda_init):
    i = pl.program_id(1)
    n_blocks = ((i + 1) * tq + tk - 1) // tk

    @pl.when(i == 0)
    def _():
        vp_ref[:, :HEAD_DIM] = v_ref[...]
        vp_ref[:, HEAD_DIM:] = jnp.ones((v_ref.shape[0], HEAD_DIM), BF16)

    q = q_ref[...]
    lane_q = lax.broadcasted_iota(jnp.int32, q.shape, 1)
    zero = jnp.zeros_like(q)
    q_st = jnp.concatenate([jnp.where(lane_q < DIFF_QK_DIM, q, zero), jnp.where(lane_q >= DIFF_QK_DIM, q, zero)],
                           axis=0)
    lane = lax.broadcasted_iota(jnp.int32, (2 * tq, LANES), 1)
    row = lax.broadcasted_iota(jnp.int32, (2 * tq, LANES), 0)
    q_pos = i * tq + jnp.where(row >= tq, row - tq, row)

    m_ref[...] = jnp.full_like(m_ref, -jnp.inf)
    acc_ref[...] = jnp.zeros_like(acc_ref)

    def key_start(b):
        return pl.multiple_of(jnp.maximum(n_blocks - 1 - b, 0) * tk, tk)

    def scores(b, slot):
        s_ref[slot] = _qk(q_st, k_ref[pl.ds(key_start(b), tk), :])

    def softmax(slot, diagonal):
        thr = q_pos - (n_blocks - 1) * tk

        def chunk(c):
            s_c = s_ref[slot, :, c * LANES:(c + 1) * LANES]
            return jnp.where(lane + c * LANES <= thr, s_c, -jnp.inf) if diagonal else s_c

        m_blk = chunk(0)
        for c in range(1, tk // LANES):
            m_blk = jnp.maximum(m_blk, chunk(c))
        m_old = m_ref[...]
        m_new = jnp.maximum(m_old, jnp.max(m_blk, axis=-1, keepdims=True))
        alpha_ref[slot] = jnp.exp2(m_old - m_new)
        m_ref[...] = m_new
        for c in range(tk // LANES):
            p_ref[slot, :, c * LANES:(c + 1) * LANES] = jnp.exp2(chunk(c) - m_new).astype(BF16)

    def weighted_sum(b, slot):
        pv = jnp.dot(p_ref[slot], vp_ref[pl.ds(key_start(b), tk), :], preferred_element_type=F32)
        alpha = alpha_ref[slot]
        acc_ref[:, :HEAD_DIM] = alpha * acc_ref[:, :HEAD_DIM] + pv[:, :HEAD_DIM]
        acc_ref[:, HEAD_DIM:] = alpha * acc_ref[:, HEAD_DIM:] + pv[:, HEAD_DIM:]

    scores(0, 0)
    scores(1, 1)
    softmax(0, True)

    def body(j, carry):
        b = 2 * j
        weighted_sum(b, 0)
        softmax(1, False)
        scores(b + 2, 0)
        weighted_sum(b + 1, 1)
        softmax(0, False)
        scores(b + 3, 1)
        return carry

    n_pairs = (n_blocks - 1) // 2
    lax.fori_loop(0, n_pairs, body, 0)
    last_is_odd = n_blocks - 2 * n_pairs == 2

    @pl.when(last_is_odd)
    def _():
        softmax(1, False)

    weighted_sum(2 * n_pairs, 0)

    @pl.when(last_is_odd)
    def _():
        weighted_sum(2 * n_pairs + 1, 1)

    lam = (jnp.exp(jnp.sum(lq1_ref[...] * lk1_ref[...], keepdims=True))
           - jnp.exp(jnp.sum(lq2_ref[...] * lk2_ref[...], keepdims=True)) + lambda_init)
    acc = acc_ref[...]
    o1 = acc[:tq, :HEAD_DIM] / acc[:tq, HEAD_DIM:]
    o2 = acc[tq:, :HEAD_DIM] / acc[tq:, HEAD_DIM:]
    out = o1 - lam * o2
    o_ref[...] = ((_rms(out) * g_ref[...]) * (1.0 - lambda_init)).astype(o_ref.dtype)


def diff_attention(qkv, lq1, lk1, lq2, lk2, subln, *, n_heads, q_col, k_col, v_col, lambda_init, tq=512, tk=512):
    s = qkv[0].shape[0]
    tq, tk = min(tq, s), min(tk, s)
    assert tk % tq == 0 and s % tk == 0
    qb, kb, vb = q_col // HEAD_DIM, k_col // HEAD_DIM, v_col // HEAD_DIM
    lam_spec = pl.BlockSpec((1, DIFF_QK_DIM), lambda h, i: (0, 0))
    return pl.pallas_call(
        functools.partial(_diff_attn_kernel, tq=tq, tk=tk, lambda_init=lambda_init),
        grid=(n_heads, s // tq),
        in_specs=[
            pl.BlockSpec((tq, HEAD_DIM), lambda h, i: (i, qb + h)),
            pl.BlockSpec((s, HEAD_DIM), lambda h, i: (0, kb + h)),
            pl.BlockSpec((s, HEAD_DIM), lambda h, i: (0, vb + h)),
            lam_spec, lam_spec, lam_spec, lam_spec,
            pl.BlockSpec((1, HEAD_DIM), lambda h, i: (0, 0)),
        ],
        out_specs=pl.BlockSpec((tq, HEAD_DIM), lambda h, i: (i, h)),
        out_shape=jax.ShapeDtypeStruct((s, n_heads * HEAD_DIM), BF16),
        scratch_shapes=[
            pltpu.VMEM((s, 2 * HEAD_DIM), BF16),
            pltpu.VMEM((2, 2 * tq, tk), F32),
            pltpu.VMEM((2, 2 * tq, tk), BF16),
            pltpu.VMEM((2, 2 * tq, LANES), F32),
            pltpu.VMEM((2 * tq, LANES), F32),
            pltpu.VMEM((2 * tq, 2 * HEAD_DIM), F32),
        ],
        compiler_params=_params("arbitrary", "arbitrary"),
        name="diff_attention",
    )(*qkv, lq1.reshape(1, -1), lk1.reshape(1, -1), lq2.reshape(1, -1), lk2.reshape(1, -1), subln.reshape(1, -1))


def _sb_attn_kernel(q_ref, k_ref, v_ref, tri_ref, g_ref, o_ref, c_ref, acc_ref, *, tq, tk, heads):
    i = pl.program_id(1)
    tri = tri_ref[...]

    def block(h, kb, masked, c):
        cols = slice(h * HEAD_DIM, (h + 1) * HEAD_DIM)
        start = pl.multiple_of(kb * tk, tk)
        k = k_ref[pl.ds(start, tk), cols]
        v = v_ref[pl.ds(start, tk), cols]
        z = _qk(q_ref[:, cols], k)
        nz = -z
        lsm = jnp.minimum(nz, 0.0) - jnp.log2(1.0 + jnp.exp2(jnp.minimum(z, nz)))
        if masked:
            q_pos = i * tq + lax.broadcasted_iota(jnp.int32, z.shape, 0)
            k_pos = kb * tk + lax.broadcasted_iota(jnp.int32, z.shape, 1)
            strict = k_pos < q_pos
            lsm = jnp.where(strict, lsm, 0.0)
        hi = lsm.astype(BF16)
        lo = (lsm - hi.astype(F32)).astype(BF16)
        r = jnp.dot(hi, tri, preferred_element_type=F32) + jnp.dot(lo, tri, preferred_element_type=F32)
        a = jnp.exp2(z + r + c)
        if masked:
            a = jnp.where(strict, a, 0.0)
        return jnp.dot(a.astype(BF16), v, preferred_element_type=F32), c + r[:, 0:1]

    n_diag = tq // tk
    kb_top = (i + 1) * n_diag - 1

    def head_blocks(with_left):
        for h in range(heads):
            c = jnp.zeros((tq, 1), F32)
            acc = None
            for d in range(n_diag + int(with_left)):
                pv, c = block(h, kb_top - d, d < n_diag, c)
                acc = pv if acc is None else acc + pv
            acc_ref[h] = acc
            c_ref[h] = c

    pl.when(i == 0)(functools.partial(head_blocks, False))
    pl.when(i > 0)(functools.partial(head_blocks, True))

    def cond(carry):
        kb, cm = carry
        return jnp.logical_and(kb >= 0, cm > -SB_EXIT * LOG2_E)

    def body(carry):
        kb, _ = carry
        for h in range(heads):
            pv, c = block(h, kb, False, c_ref[h])
            acc_ref[h] += pv
            c_ref[h] = c
        return kb - 1, jnp.max(c_ref[...])

    lax.while_loop(cond, body, (kb_top - n_diag - 1, jnp.max(c_ref[...])))
    for h in range(heads):
        o_ref[:, h * HEAD_DIM:(h + 1) * HEAD_DIM] = (_rms(acc_ref[h]) * g_ref[...]).astype(o_ref.dtype)


def sb_attention(qkv, gain, *, n_heads, q_col, k_col, v_col, tq=256, tk=256, heads=4):
    s = qkv[0].shape[0]
    tq, tk, heads = min(tq, s), min(tk, s), min(heads, n_heads)
    width = heads * HEAD_DIM
    assert tq % tk == 0 and n_heads % heads == 0 and all(c % width == 0 for c in (q_col, k_col, v_col))
    qb, kb, vb = q_col // width, k_col // width, v_col // width
    tri = (lax.broadcasted_iota(jnp.int32, (tk, tk), 0) >= lax.broadcasted_iota(jnp.int32, (tk, tk), 1)).astype(BF16)
    return pl.pallas_call(
        functools.partial(_sb_attn_kernel, tq=tq, tk=tk, heads=heads),
        grid=(n_heads // heads, s // tq),
        in_specs=[
            pl.BlockSpec((tq, width), lambda g, i: (i, qb + g)),
            pl.BlockSpec((s, width), lambda g, i: (0, kb + g)),
            pl.BlockSpec((s, width), lambda g, i: (0, vb + g)),
            pl.BlockSpec((tk, tk), lambda g, i: (0, 0)),
            pl.BlockSpec((1, HEAD_DIM), lambda g, i: (0, 0)),
        ],
        out_specs=pl.BlockSpec((tq, width), lambda g, i: (i, g)),
        out_shape=jax.ShapeDtypeStruct((s, n_heads * HEAD_DIM), BF16),
        scratch_shapes=[pltpu.VMEM((heads, tq, 1), F32), pltpu.VMEM((heads, tq, HEAD_DIM), F32)],
        compiler_params=_params("parallel", "arbitrary"),
        name="sb_attention",
    )(*qkv, tri, gain.reshape(1, -1))


def _mem_kv_kernel(mem_ref, g_ref, w_ref, o_ref):
    mn = (_rms(mem_ref[...]) * g_ref[...]).astype(BF16)
    o_ref[...] = jnp.dot(mn, w_ref[...], preferred_element_type=F32).astype(o_ref.dtype)


def mem_kv(mem, gain, w_kv):
    mt, d = mem.shape
    n = w_kv.shape[1]
    return pl.pallas_call(
        _mem_kv_kernel,
        grid=(1,),
        in_specs=[pl.BlockSpec((mt, d), lambda i: (0, 0)), pl.BlockSpec((1, d), lambda i: (0, 0)),
                  pl.BlockSpec((d, n), lambda i: (0, 0))],
        out_specs=pl.BlockSpec((mt, n), lambda i: (0, 0)),
        out_shape=jax.ShapeDtypeStruct((mt, n), BF16),
        compiler_params=_params("arbitrary"),
        name="mem_kv",
    )(mem, gain.reshape(1, d), w_kv)


def _xattn_kernel(h_ref, y_ref, gmix_ref, gpre_ref, wq_ref, kv_ref, wo_ref, gpost_ref, gnext_ref, hout_ref, hn_ref, *,
                  scale):
    h = h_ref[...] + _rms(y_ref[...]) * gmix_ref[...]
    hn = (_rms(h) * gpre_ref[...]).astype(BF16)
    q = jnp.dot(hn, wq_ref[...], preferred_element_type=F32).astype(BF16)
    width = XATTN_HEADS * HEAD_DIM
    outs = []
    for hd in range(XATTN_HEADS):
        cols = slice(hd * HEAD_DIM, (hd + 1) * HEAD_DIM)
        k = kv_ref[:, cols]
        v = kv_ref[:, width + hd * HEAD_DIM:width + (hd + 1) * HEAD_DIM]
        sc = _qk(q[:, cols], k) * scale
        sc = sc - jnp.max(sc, axis=-1, keepdims=True)
        e = jnp.exp(sc)
        p = e / jnp.sum(e, axis=-1, keepdims=True)
        outs.append(jnp.dot(p.astype(BF16), v, preferred_element_type=F32).astype(BF16))
    o = jnp.concatenate(outs, axis=-1)
    y = jnp.dot(o, wo_ref[...], preferred_element_type=F32)
    h2 = h + _rms(y) * gpost_ref[...]
    hout_ref[...] = h2
    hn_ref[...] = (_rms(h2) * gnext_ref[...]).astype(hn_ref.dtype)


def xattn_block(h, y_mix, gmix, gpre, wq, kv, wo, gpost, gnext, *, tm=256):
    m, d = h.shape
    tm = min(tm, m)
    width = wq.shape[1]
    row = pl.BlockSpec((tm, d), lambda i: (i, 0))
    vec = pl.BlockSpec((1, d), lambda i: (0, 0))
    return pl.pallas_call(
        functools.partial(_xattn_kernel, scale=HEAD_DIM ** -0.5),
        grid=(m // tm,),
        in_specs=[row, row, vec, vec, pl.BlockSpec((d, width), lambda i: (0, 0)),
                  pl.BlockSpec(kv.shape, lambda i: (0, 0)), pl.BlockSpec((width, d), lambda i: (0, 0)), vec, vec],
        out_specs=[row, row],
        out_shape=[jax.ShapeDtypeStruct((m, d), F32), jax.ShapeDtypeStruct((m, d), BF16)],
        compiler_params=_params("parallel"),
        name="xattn_block",
    )(h, y_mix, gmix.reshape(1, d), gpre.reshape(1, d), wq, kv, wo, gpost.reshape(1, d), gnext.reshape(1, d))


def kernel(x, mem, positions, ffn1_norm_pre, ffn1_norm_post, ffn1_w_gate, ffn1_w_up, ffn1_w_down, mix_norm_pre, mix_norm_post, w_in, w_out, lambda_q1, lambda_k1, lambda_q2, lambda_k2, diff_subln, sb_norm, xattn_norm_pre, xattn_norm_post, mem_norm, xattn_w_q, xattn_w_kv, xattn_w_o, ffn2_norm_pre, ffn2_norm_post, ffn2_w_gate, ffn2_w_up, ffn2_w_down):
    b, s, d = x.shape
    depth = w_in.shape[0]
    in_width = w_in.shape[2]
    group = in_width // 6
    n_heads = group // HEAD_DIM
    outs = []
    for bi in range(b):
        h = x[bi]
        pos_col = positions[bi].astype(F32).reshape(s, 1)
        ct, s1, s2 = rope_tables(pos_col)
        hn = prenorm(h, ffn1_norm_pre[0])
        for l in range(depth):
            lambda_init = 0.8 - 0.6 * math.exp(-0.3 * l)
            cast = lambda w: w[l].astype(BF16)
            act = ffn_up(hn, ffn1_w_gate[l], ffn1_w_up[l])
            y = matmul_kblocked(act, ffn1_w_down[l])
            h, hn = post_pre(h, y, ffn1_norm_post[l], mix_norm_pre[l], 0.5)
            qk, rest = in_proj(hn, w_in[l], ct, s1, s2, rope_cols=2 * group, scaled_cols=group,
                               q_scale=DIFF_QK_DIM ** -0.5 * LOG2_E, sb_q_cols=(3 * group, 4 * group),
                               sb_q_scale=HEAD_DIM ** -0.5 * LOG2_E)
            diff_out = diff_attention((qk, qk, rest), lambda_q1[l], lambda_k1[l], lambda_q2[l], lambda_k2[l],
                                      diff_subln[l], n_heads=n_heads, q_col=0, k_col=group, v_col=0,
                                      lambda_init=lambda_init)
            sb_out = sb_attention((rest, rest, rest), sb_norm[l], n_heads=n_heads, q_col=group, k_col=2 * group,
                                  v_col=3 * group)
            y = out_proj(diff_out, sb_out, cast(w_out))
            kv = mem_kv(mem[bi], mem_norm[l], cast(xattn_w_kv))
            h, hn = xattn_block(h, y, mix_norm_post[l], xattn_norm_pre[l], cast(xattn_w_q), kv, cast(xattn_w_o),
                                xattn_norm_post[l], ffn2_norm_pre[l])
            act = ffn_up(hn, ffn2_w_gate[l], ffn2_w_up[l])
            y = matmul_kblocked(act, ffn2_w_down[l])
            if l + 1 < depth:
                h, hn = post_pre(h, y, ffn2_norm_post[l], ffn1_norm_pre[l + 1], 0.5)
            else:
                h = post_pre(h, y, ffn2_norm_post[l], None, 0.5)
        outs.append(h)
    return jnp.stack(outs, axis=0)
```

```python
import functools
import math

import jax
import jax.numpy as jnp
from jax import lax
from jax.experimental import pallas as pl
from jax.experimental.pallas import tpu as pltpu

F32 = jnp.float32
BF16 = jnp.bfloat16

HEAD_DIM = 128
DIFF_QK_DIM = 64
ROT_DIM = 16
ROPE_THETA = 500000.0
NORM_EPS = 1e-6
XATTN_HEADS = 4
LANES = 128

VMEM_LIMIT_BYTES = 56 * 1024 * 1024
SB_EXIT = 100.0
LOG2_E = math.log2(math.e)


def _params(*sem):
    return pltpu.CompilerParams(dimension_semantics=sem, vmem_limit_bytes=VMEM_LIMIT_BYTES)


def _rms(x):
    return x * lax.rsqrt(jnp.mean(x * x, axis=-1, keepdims=True) + NORM_EPS)


def _prenorm_kernel(x_ref, g_ref, o_ref):
    o_ref[...] = (_rms(x_ref[...]) * g_ref[...]).astype(o_ref.dtype)


def prenorm(x, gain, *, tm=256):
    m, d = x.shape
    return pl.pallas_call(
        _prenorm_kernel,
        grid=(m // tm,),
        in_specs=[pl.BlockSpec((tm, d), lambda i: (i, 0)), pl.BlockSpec((1, d), lambda i: (0, 0))],
        out_specs=pl.BlockSpec((tm, d), lambda i: (i, 0)),
        out_shape=jax.ShapeDtypeStruct((m, d), BF16),
        compiler_params=_params("parallel"),
        name="prenorm",
    )(x, gain.reshape(1, d))


def _post_pre_kernel(h_ref, y_ref, gpost_ref, gpre_ref, hout_ref, hn_ref, *, coef):
    h = h_ref[...] + coef * (_rms(y_ref[...]) * gpost_ref[...])
    hout_ref[...] = h
    hn_ref[...] = (_rms(h) * gpre_ref[...]).astype(hn_ref.dtype)


def _post_kernel(h_ref, y_ref, gpost_ref, hout_ref, *, coef):
    hout_ref[...] = h_ref[...] + coef * (_rms(y_ref[...]) * gpost_ref[...])


def post_pre(h, y, gpost, gpre, coef, *, tm=256):
    m, d = h.shape
    row = pl.BlockSpec((tm, d), lambda i: (i, 0))
    vec = pl.BlockSpec((1, d), lambda i: (0, 0))
    if gpre is None:
        return pl.pallas_call(
            functools.partial(_post_kernel, coef=coef),
            grid=(m // tm,),
            in_specs=[row, row, vec],
            out_specs=row,
            out_shape=jax.ShapeDtypeStruct((m, d), F32),
            compiler_params=_params("parallel"),
            name="post_norm",
        )(h, y, gpost.reshape(1, d))
    return pl.pallas_call(
        functools.partial(_post_pre_kernel, coef=coef),
        grid=(m // tm,),
        in_specs=[row, row, vec, vec],
        out_specs=[row, row],
        out_shape=[jax.ShapeDtypeStruct((m, d), F32), jax.ShapeDtypeStruct((m, d), BF16)],
        compiler_params=_params("parallel"),
        name="post_pre_norm",
    )(h, y, gpost.reshape(1, d), gpre.reshape(1, d))


CAST_ROWS = 512


def _cast_weight_once(w_ref, w_bf):
    @pl.when(pl.program_id(1) == 0)
    def _():
        for r in range(0, w_ref.shape[0], CAST_ROWS):
            w_bf[r:r + CAST_ROWS, :] = w_ref[r:r + CAST_ROWS, :].astype(BF16)


def _ffn_up_kernel(x_ref, wg_ref, wu_ref, o_ref, wg_bf, wu_bf):
    _cast_weight_once(wg_ref, wg_bf)
    _cast_weight_once(wu_ref, wu_bf)
    x = x_ref[...]
    g = jnp.dot(x, wg_bf[...], preferred_element_type=F32)
    u = jnp.dot(x, wu_bf[...], preferred_element_type=F32)
    o_ref[...] = (g * jax.nn.sigmoid(g) * u).astype(o_ref.dtype)


def ffn_up(xn, wg, wu, *, tm=512, tn=512):
    m, k = xn.shape
    n = wg.shape[1]
    tm, tn = min(tm, m), min(tn, n)
    assert k % CAST_ROWS == 0
    return pl.pallas_call(
        _ffn_up_kernel,
        grid=(n // tn, m // tm),
        in_specs=[
            pl.BlockSpec((tm, k), lambda j, i: (i, 0)),
            pl.BlockSpec((k, tn), lambda j, i: (0, j)),
            pl.BlockSpec((k, tn), lambda j, i: (0, j)),
        ],
        out_specs=pl.BlockSpec((tm, tn), lambda j, i: (i, j)),
        out_shape=jax.ShapeDtypeStruct((m, n), BF16),
        scratch_shapes=[pltpu.VMEM((k, tn), BF16), pltpu.VMEM((k, tn), BF16)],
        compiler_params=_params("arbitrary", "arbitrary"),
        name="ffn_up",
    )(xn, wg, wu)


def _mm_acc_kernel(a_ref, b_ref, o_ref):
    @pl.when(pl.program_id(2) == 0)
    def _():
        o_ref[...] = jnp.zeros_like(o_ref)

    o_ref[...] += jnp.dot(a_ref[...], b_ref[...].astype(BF16), preferred_element_type=F32)


def matmul_kblocked(a, b, *, tm=2048, tn=1024, tk=1024):
    m, k = a.shape
    n = b.shape[1]
    tm, tn, tk = min(tm, m), min(tn, n), min(tk, k)
    return pl.pallas_call(
        _mm_acc_kernel,
        grid=(m // tm, n // tn, k // tk),
        in_specs=[
            pl.BlockSpec((tm, tk), lambda i, j, l: (i, l)),
            pl.BlockSpec((tk, tn), lambda i, j, l: (l, j)),
        ],
        out_specs=pl.BlockSpec((tm, tn), lambda i, j, l: (i, j)),
        out_shape=jax.ShapeDtypeStruct((m, n), F32),
        compiler_params=_params("parallel", "parallel", "arbitrary"),
        name="matmul_kblocked",
    )(a, b)


def _out_proj_kernel(a_ref, b_ref, wa_ref, wb_ref, o_ref):
    o_ref[...] = jnp.dot(a_ref[...], wa_ref[...], preferred_element_type=F32) + jnp.dot(
        b_ref[...], wb_ref[...], preferred_element_type=F32
    )


def out_proj(a, b, w, *, tm=1024, tn=1024):
    m, ka = a.shape
    kb = b.shape[1]
    n = w.shape[1]
    tm, tn = min(tm, m), min(tn, n)
    assert ka == kb
    return pl.pallas_call(
        _out_proj_kernel,
        grid=(m // tm, n // tn),
        in_specs=[
            pl.BlockSpec((tm, ka), lambda i, j: (i, 0)),
            pl.BlockSpec((tm, kb), lambda i, j: (i, 0)),
            pl.BlockSpec((ka, tn), lambda i, j: (0, j)),
            pl.BlockSpec((kb, tn), lambda i, j: (1, j)),
        ],
        out_specs=pl.BlockSpec((tm, tn), lambda i, j: (i, j)),
        out_shape=jax.ShapeDtypeStruct((m, n), F32),
        compiler_params=_params("parallel", "arbitrary"),
        name="out_proj",
    )(a, b, w, w)


def _rope_table_kernel(pos_ref, invf_ref, ct_ref, s1_ref, s2_ref):
    ang = pos_ref[...] * invf_ref[...]
    cos, sin = jnp.cos(ang), jnp.sin(ang)
    lane = lax.broadcasted_iota(jnp.int32, ang.shape, 1) % DIFF_QK_DIM
    half = ROT_DIM // 2
    ct_ref[...] = jnp.where(lane < ROT_DIM, cos, 1.0)
    s1_ref[...] = jnp.where(lane < half, -sin, 0.0)
    s2_ref[...] = jnp.where((lane >= half) & (lane < ROT_DIM), sin, 0.0)


def rope_tables(positions_col, *, tm=1024):
    s = positions_col.shape[0]
    tm = min(tm, s)
    half = ROT_DIM // 2
    inv_freq = ROPE_THETA ** (-jnp.arange(0, ROT_DIM, 2, dtype=F32) / ROT_DIM)
    invf = jnp.tile(inv_freq, LANES // half).reshape(1, LANES)
    tab = pl.BlockSpec((tm, LANES), lambda i: (i, 0))
    return pl.pallas_call(
        _rope_table_kernel,
        grid=(s // tm,),
        in_specs=[pl.BlockSpec((tm, 1), lambda i: (i, 0)), pl.BlockSpec((1, LANES), lambda i: (0, 0))],
        out_specs=[tab, tab, tab],
        out_shape=[jax.ShapeDtypeStruct((s, LANES), F32)] * 3,
        compiler_params=_params("parallel"),
        name="rope_tables",
    )(positions_col, invf)


def _in_proj_plain_kernel(x_ref, w_ref, o_ref, w_bf, *, scaled_tiles, scale):
    _cast_weight_once(w_ref, w_bf)
    j = pl.program_id(0)
    s = jnp.where((j >= scaled_tiles[0]) & (j < scaled_tiles[1]), scale, 1.0).astype(F32)
    o_ref[...] = (jnp.dot(x_ref[...], w_bf[...], preferred_element_type=F32) * s).astype(o_ref.dtype)


def _in_proj_rope_kernel(x_ref, w_ref, ct_ref, s1_ref, s2_ref, o_ref, w_bf, *, n_scaled_tiles, q_scale):
    _cast_weight_once(w_ref, w_bf)
    acc = jnp.dot(x_ref[...], w_bf[...], preferred_element_type=F32)
    scale = jnp.where(pl.program_id(0) < n_scaled_tiles, q_scale, 1.0).astype(F32)
    ct, s1, s2 = ct_ref[...], s1_ref[...], s2_ref[...]
    half = ROT_DIM // 2
    for c in range(acc.shape[1] // LANES):
        a = acc[:, c * LANES:(c + 1) * LANES]
        r = a * ct + pltpu.roll(a, LANES - half, 1) * s1 + pltpu.roll(a, half, 1) * s2
        o_ref[:, c * LANES:(c + 1) * LANES] = (r * scale).astype(o_ref.dtype)


def in_proj(xn, w, ct, s1, s2, *, rope_cols, scaled_cols, q_scale, sb_q_cols, sb_q_scale, tm=1024, tn=512):
    m, k = xn.shape
    n = w.shape[1]
    tm, tn = min(tm, m), min(tn, n, scaled_cols)
    assert rope_cols % tn == 0 and scaled_cols % tn == 0 and k % CAST_ROWS == 0
    assert all((c - rope_cols) % tn == 0 for c in sb_q_cols)
    sb_tiles = tuple((c - rope_cols) // tn for c in sb_q_cols)
    n_rope_tiles = rope_cols // tn
    x_spec = pl.BlockSpec((tm, k), lambda j, i: (i, 0))
    out_spec = pl.BlockSpec((tm, tn), lambda j, i: (i, j))
    tab = pl.BlockSpec((tm, LANES), lambda j, i: (i, 0))
    scratch = [pltpu.VMEM((k, tn), BF16)]
    roped = pl.pallas_call(
        functools.partial(_in_proj_rope_kernel, n_scaled_tiles=scaled_cols // tn, q_scale=q_scale),
        grid=(n_rope_tiles, m // tm),
        in_specs=[x_spec, pl.BlockSpec((k, tn), lambda j, i: (0, j)), tab, tab, tab],
        out_specs=out_spec,
        out_shape=jax.ShapeDtypeStruct((m, rope_cols), BF16),
        scratch_shapes=scratch,
        compiler_params=_params("arbitrary", "arbitrary"),
        name="in_proj_rope",
    )(xn, w, ct, s1, s2)
    plain = pl.pallas_call(
        functools.partial(_in_proj_plain_kernel, scaled_tiles=sb_tiles, scale=sb_q_scale),
        grid=((n - rope_cols) // tn, m // tm),
        in_specs=[x_spec, pl.BlockSpec((k, tn), lambda j, i: (0, j + n_rope_tiles))],
        out_specs=out_spec,
        out_shape=jax.ShapeDtypeStruct((m, n - rope_cols), BF16),
        scratch_shapes=scratch,
        compiler_params=_params("arbitrary", "arbitrary"),
        name="in_proj_plain",
    )(xn, w)
    return roped, plain


def _qk(q, k):
    return lax.dot_general(q, k, (((1,), (1,)), ((), ())), preferred_element_type=F32)


def _diff_attn_kernel(q_ref, k_ref, v_ref, lq1_ref, lk1_ref, lq2_ref, lk2_ref, g_ref, o_ref,
                      vp_ref, s_ref, p_ref, alpha_ref, m_ref, acc_ref, *, tq, lambda_init):
    tk = tq
    pair = pl.program_id(1)

    @pl.when(pair == 0)
    def _():
        vp_ref[:, :HEAD_DIM] = v_ref[...]
        vp_ref[:, HEAD_DIM:] = jnp.ones((v_ref.shape[0], HEAD_DIM), BF16)

    lane_q = lax.broadcasted_iota(jnp.int32, (tq, HEAD_DIM), 1)
    lane = lax.broadcasted_iota(jnp.int32, (2 * tq, LANES), 1)
    row = lax.broadcasted_iota(jnp.int32, (2 * tq, LANES), 0)
    row_in_tile = jnp.where(row >= tq, row - tq, row)

    class Tile:
        def __init__(self, t):
            self.t = t
            self.n_blocks = 2 * pair + t + 1
            q = q_ref[t * tq:(t + 1) * tq, :]
            zero = jnp.zeros_like(q)
            self.q_st = jnp.concatenate(
                [jnp.where(lane_q < DIFF_QK_DIM, q, zero), jnp.where(lane_q >= DIFF_QK_DIM, q, zero)], axis=0)
            m_ref[t] = jnp.full((2 * tq, LANES), -jnp.inf, F32)
            acc_ref[t] = jnp.zeros((2 * tq, 2 * HEAD_DIM), F32)

        def key_start(self, b):
            return pl.multiple_of(jnp.maximum(self.n_blocks - 1 - b, 0) * tk, tk)

        def scores(self, b, slot):
            s_ref[self.t, slot] = _qk(self.q_st, k_ref[pl.ds(self.key_start(b), tk), :])

        def softmax(self, slot, diagonal):
            t = self.t

            def chunk(c):
                s_c = s_ref[t, slot, :, c * LANES:(c + 1) * LANES]
                return jnp.where(lane + c * LANES <= row_in_tile, s_c, -jnp.inf) if diagonal else s_c

            m_blk = chunk(0)
            for c in range(1, tk // LANES):
                m_blk = jnp.maximum(m_blk, chunk(c))
            m_old = m_ref[t]
            m_new = jnp.maximum(m_old, jnp.max(m_blk, axis=-1, keepdims=True))
            alpha_ref[t, slot] = jnp.exp2(m_old - m_new)
            m_ref[t] = m_new
            for c in range(tk // LANES):
                p_ref[t, slot, :, c * LANES:(c + 1) * LANES] = jnp.exp2(chunk(c) - m_new).astype(BF16)

        def weighted_sum(self, b, slot):
            t = self.t
            pv = jnp.dot(p_ref[t, slot], vp_ref[pl.ds(self.key_start(b), tk), :],
                         preferred_element_type=F32)
            alpha = alpha_ref[t, slot]
            acc_ref[t, :, :HEAD_DIM] = alpha * acc_ref[t, :, :HEAD_DIM] + pv[:, :HEAD_DIM]
            acc_ref[t, :, HEAD_DIM:] = alpha * acc_ref[t, :, HEAD_DIM:] + pv[:, HEAD_DIM:]

        def finish(self, lam):
            acc = acc_ref[self.t]
            o1 = acc[:tq, :HEAD_DIM] / acc[:tq, HEAD_DIM:]
            o2 = acc[tq:, :HEAD_DIM] / acc[tq:, HEAD_DIM:]
            out = o1 - lam * o2
            o_ref[self.t * tq:(self.t + 1) * tq, :] = (
                (_rms(out) * g_ref[...]) * (1.0 - lambda_init)).astype(o_ref.dtype)

    tiles = (Tile(0), Tile(1))
    for tl in tiles:
        tl.scores(0, 0)
    for tl in tiles:
        tl.scores(1, 1)
    for tl in tiles:
        tl.softmax(0, True)

    def body(j, carry):
        b = 2 * j
        for tl in tiles:
            tl.weighted_sum(b, 0)
            tl.softmax(1, False)
            tl.scores(b + 2, 0)
        for tl in tiles:
            tl.weighted_sum(b + 1, 1)
            tl.softmax(0, False)
            tl.scores(b + 3, 1)
        return carry

    lax.fori_loop(0, pair, body, 0)
    last = 2 * pair
    tiles[1].softmax(1, False)
    for tl in tiles:
        tl.weighted_sum(last, 0)
    tiles[1].weighted_sum(last + 1, 1)

    lam = (jnp.exp(jnp.sum(lq1_ref[...] * lk1_ref[...], keepdims=True))
           - jnp.exp(jnp.sum(lq2_ref[...] * lk2_ref[...], keepdims=True)) + lambda_init)
    for tl in tiles:
        tl.finish(lam)


def diff_attention(qkv, lq1, lk1, lq2, lk2, subln, *, n_heads, q_col, k_col, v_col, lambda_init, tq=512):
    s = qkv[0].shape[0]
    tq = min(tq, s // 2)
    assert s % (2 * tq) == 0
    qb, kb, vb = q_col // HEAD_DIM, k_col // HEAD_DIM, v_col // HEAD_DIM
    lam_spec = pl.BlockSpec((1, DIFF_QK_DIM), lambda h, i: (0, 0))
    return pl.pallas_call(
        functools.partial(_diff_attn_kernel, tq=tq, lambda_init=lambda_init),
        grid=(n_heads, s // (2 * tq)),
        in_specs=[
            pl.BlockSpec((2 * tq, HEAD_DIM), lambda h, i: (i, qb + h)),
            pl.BlockSpec((s, HEAD_DIM), lambda h, i: (0, kb + h)),
            pl.BlockSpec((s, HEAD_DIM), lambda h, i: (0, vb + h)),
            lam_spec, lam_spec, lam_spec, lam_spec,
            pl.BlockSpec((1, HEAD_DIM), lambda h, i: (0, 0)),
        ],
        out_specs=pl.BlockSpec((2 * tq, HEAD_DIM), lambda h, i: (i, h)),
        out_shape=jax.ShapeDtypeStruct((s, n_heads * HEAD_DIM), BF16),
        scratch_shapes=[
            pltpu.VMEM((s, 2 * HEAD_DIM), BF16),
            pltpu.VMEM((2, 2, 2 * tq, tq), F32),
            pltpu.VMEM((2, 2, 2 * tq, tq), BF16),
            pltpu.VMEM((2, 2, 2 * tq, LANES), F32),
            pltpu.VMEM((2, 2 * tq, LANES), F32),
            pltpu.VMEM((2, 2 * tq, 2 * HEAD_DIM), F32),
        ],
        compiler_params=_params("arbitrary", "arbitrary"),
        name="diff_attention",
    )(*qkv, lq1.reshape(1, -1), lk1.reshape(1, -1), lq2.reshape(1, -1), lk2.reshape(1, -1), subln.reshape(1, -1))


def _sb_attn_kernel(q_ref, k_ref, v_ref, tri_ref, g_ref, o_ref, c_ref, acc_ref, *, tq, tk, heads):
    i = pl.program_id(1)
    tri = tri_ref[...]

    def block(h, kb, masked, c):
        cols = slice(h * HEAD_DIM, (h + 1) * HEAD_DIM)
        start = pl.multiple_of(kb * tk, tk)
        k = k_ref[pl.ds(start, tk), cols]
        v = v_ref[pl.ds(start, tk), cols]
        z = _qk(q_ref[:, cols], k)
        nz = -z
        lsm = jnp.minimum(nz, 0.0) - jnp.log2(1.0 + jnp.exp2(jnp.minimum(z, nz)))
        if masked:
            q_pos = i * tq + lax.broadcasted_iota(jnp.int32, z.shape, 0)
            k_pos = kb * tk + lax.broadcasted_iota(jnp.int32, z.shape, 1)
            strict = k_pos < q_pos
            lsm = jnp.where(strict, lsm, 0.0)
        hi = lsm.astype(BF16)
        lo = (lsm - hi.astype(F32)).astype(BF16)
        r = jnp.dot(hi, tri, preferred_element_type=F32) + jnp.dot(lo, tri, preferred_element_type=F32)
        a = jnp.exp2(z + r + c)
        if masked:
            a = jnp.where(strict, a, 0.0)
        return jnp.dot(a.astype(BF16), v, preferred_element_type=F32), c + r[:, 0:1]

    n_diag = tq // tk
    kb_top = (i + 1) * n_diag - 1

    def head_blocks(with_left):
        for h in range(heads):
            c = jnp.zeros((tq, 1), F32)
            acc = None
            for d in range(n_diag + int(with_left)):
                pv, c = block(h, kb_top - d, d < n_diag, c)
                acc = pv if acc is None else acc + pv
            acc_ref[h] = acc
            c_ref[h] = c

    pl.when(i == 0)(functools.partial(head_blocks, False))
    pl.when(i > 0)(functools.partial(head_blocks, True))

    def cond(carry):
        kb, cm = carry
        return jnp.logical_and(kb >= 0, cm > -SB_EXIT * LOG2_E)

    def body(carry):
        kb, _ = carry
        for h in range(heads):
            pv, c = block(h, kb, False, c_ref[h])
            acc_ref[h] += pv
            c_ref[h] = c
        return kb - 1, jnp.max(c_ref[...])

    lax.while_loop(cond, body, (kb_top - n_diag - 1, jnp.max(c_ref[...])))
    for h in range(heads):
        o_ref[:, h * HEAD_DIM:(h + 1) * HEAD_DIM] = (_rms(acc_ref[h]) * g_ref[...]).astype(o_ref.dtype)


def sb_attention(qkv, gain, *, n_heads, q_col, k_col, v_col, tq=256, tk=256, heads=4):
    s = qkv[0].shape[0]
    tq, tk, heads = min(tq, s), min(tk, s), min(heads, n_heads)
    width = heads * HEAD_DIM
    assert tq % tk == 0 and n_heads % heads == 0 and all(c % width == 0 for c in (q_col, k_col, v_col))
    qb, kb, vb = q_col // width, k_col // width, v_col // width
    tri = (lax.broadcasted_iota(jnp.int32, (tk, tk), 0) >= lax.broadcasted_iota(jnp.int32, (tk, tk), 1)).astype(BF16)
    return pl.pallas_call(
        functools.partial(_sb_attn_kernel, tq=tq, tk=tk, heads=heads),
        grid=(n_heads // heads, s // tq),
        in_specs=[
            pl.BlockSpec((tq, width), lambda g, i: (i, qb + g)),
            pl.BlockSpec((s, width), lambda g, i: (0, kb + g)),
            pl.BlockSpec((s, width), lambda g, i: (0, vb + g)),
            pl.BlockSpec((tk, tk), lambda g, i: (0, 0)),
            pl.BlockSpec((1, HEAD_DIM), lambda g, i: (0, 0)),
        ],
        out_specs=pl.BlockSpec((tq, width), lambda g, i: (i, g)),
        out_shape=jax.ShapeDtypeStruct((s, n_heads * HEAD_DIM), BF16),
        scratch_shapes=[pltpu.VMEM((heads, tq, 1), F32), pltpu.VMEM((heads, tq, HEAD_DIM), F32)],
        compiler_params=_params("parallel", "arbitrary"),
        name="sb_attention",
    )(*qkv, tri, gain.reshape(1, -1))


def _mem_kv_kernel(mem_ref, g_ref, w_ref, o_ref):
    mn = (_rms(mem_ref[...]) * g_ref[...]).astype(BF16)
    o_ref[...] = jnp.dot(mn, w_ref[...], preferred_element_type=F32).astype(o_ref.dtype)


def mem_kv(mem, gain, w_kv):
    mt, d = mem.shape
    n = w_kv.shape[1]
    return pl.pallas_call(
        _mem_kv_kernel,
        grid=(1,),
        in_specs=[pl.BlockSpec((mt, d), lambda i: (0, 0)), pl.BlockSpec((1, d), lambda i: (0, 0)),
                  pl.BlockSpec((d, n), lambda i: (0, 0))],
        out_specs=pl.BlockSpec((mt, n), lambda i: (0, 0)),
        out_shape=jax.ShapeDtypeStruct((mt, n), BF16),
        compiler_params=_params("arbitrary"),
        name="mem_kv",
    )(mem, gain.reshape(1, d), w_kv)


def _xattn_kernel(h_ref, y_ref, gmix_ref, gpre_ref, wq_ref, kv_ref, wo_ref, gpost_ref, gnext_ref, hout_ref, hn_ref, *,
                  scale):
    h = h_ref[...] + _rms(y_ref[...]) * gmix_ref[...]
    hn = (_rms(h) * gpre_ref[...]).astype(BF16)
    q = jnp.dot(hn, wq_ref[...], preferred_element_type=F32).astype(BF16)
    width = XATTN_HEADS * HEAD_DIM
    outs = []
    for hd in range(XATTN_HEADS):
        cols = slice(hd * HEAD_DIM, (hd + 1) * HEAD_DIM)
        k = kv_ref[:, cols]
        v = kv_ref[:, width + hd * HEAD_DIM:width + (hd + 1) * HEAD_DIM]
        sc = _qk(q[:, cols], k) * scale
        sc = sc - jnp.max(sc, axis=-1, keepdims=True)
        e = jnp.exp(sc)
        p = e / jnp.sum(e, axis=-1, keepdims=True)
        outs.append(jnp.dot(p.astype(BF16), v, preferred_element_type=F32).astype(BF16))
    o = jnp.concatenate(outs, axis=-1)
    y = jnp.dot(o, wo_ref[...], preferred_element_type=F32)
    h2 = h + _rms(y) * gpost_ref[...]
    hout_ref[...] = h2
    hn_ref[...] = (_rms(h2) * gnext_ref[...]).astype(hn_ref.dtype)


def xattn_block(h, y_mix, gmix, gpre, wq, kv, wo, gpost, gnext, *, tm=256):
    m, d = h.shape
    tm = min(tm, m)
    width = wq.shape[1]
    row = pl.BlockSpec((tm, d), lambda i: (i, 0))
    vec = pl.BlockSpec((1, d), lambda i: (0, 0))
    return pl.pallas_call(
        functools.partial(_xattn_kernel, scale=HEAD_DIM ** -0.5),
        grid=(m // tm,),
        in_specs=[row, row, vec, vec, pl.BlockSpec((d, width), lambda i: (0, 0)),
                  pl.BlockSpec(kv.shape, lambda i: (0, 0)), pl.BlockSpec((width, d), lambda i: (0, 0)), vec, vec],
        out_specs=[row, row],
        out_shape=[jax.ShapeDtypeStruct((m, d), F32), jax.ShapeDtypeStruct((m, d), BF16)],
        compiler_params=_params("parallel"),
        name="xattn_block",
    )(h, y_mix, gmix.reshape(1, d), gpre.reshape(1, d), wq, kv, wo, gpost.reshape(1, d), gnext.reshape(1, d))


def kernel(x, mem, positions, ffn1_norm_pre, ffn1_norm_post, ffn1_w_gate, ffn1_w_up, ffn1_w_down, mix_norm_pre, mix_norm_post, w_in, w_out, lambda_q1, lambda_k1, lambda_q2, lambda_k2, diff_subln, sb_norm, xattn_norm_pre, xattn_norm_post, mem_norm, xattn_w_q, xattn_w_kv, xattn_w_o, ffn2_norm_pre, ffn2_norm_post, ffn2_w_gate, ffn2_w_up, ffn2_w_down):
    b, s, d = x.shape
    depth = w_in.shape[0]
    in_width = w_in.shape[2]
    group = in_width // 6
    n_heads = group // HEAD_DIM
    outs = []
    for bi in range(b):
        h = x[bi]
        pos_col = positions[bi].astype(F32).reshape(s, 1)
        ct, s1, s2 = rope_tables(pos_col)
        hn = prenorm(h, ffn1_norm_pre[0])
        for l in range(depth):
            lambda_init = 0.8 - 0.6 * math.exp(-0.3 * l)
            cast = lambda w: w[l].astype(BF16)
            act = ffn_up(hn, ffn1_w_gate[l], ffn1_w_up[l])
            y = matmul_kblocked(act, ffn1_w_down[l])
            h, hn = post_pre(h, y, ffn1_norm_post[l], mix_norm_pre[l], 0.5)
            qk, rest = in_proj(hn, w_in[l], ct, s1, s2, rope_cols=2 * group, scaled_cols=group,
                               q_scale=DIFF_QK_DIM ** -0.5 * LOG2_E, sb_q_cols=(3 * group, 4 * group),
                               sb_q_scale=HEAD_DIM ** -0.5 * LOG2_E)
            diff_out = diff_attention((qk, qk, rest), lambda_q1[l], lambda_k1[l], lambda_q2[l], lambda_k2[l],
                                      diff_subln[l], n_heads=n_heads, q_col=0, k_col=group, v_col=0,
                                      lambda_init=lambda_init)
            sb_out = sb_attention((rest, rest, rest), sb_norm[l], n_heads=n_heads, q_col=group, k_col=2 * group,
                                  v_col=3 * group)
            y = out_proj(diff_out, sb_out, cast(w_out))
            kv = mem_kv(mem[bi], mem_norm[l], cast(xattn_w_kv))
            h, hn = xattn_block(h, y, mix_norm_post[l], xattn_norm_pre[l], cast(xattn_w_q), kv, cast(xattn_w_o),
                                xattn_norm_post[l], ffn2_norm_pre[l])
            act = ffn_up(hn, ffn2_w_gate[l], ffn2_w_up[l])
            y = matmul_kblocked(act, ffn2_w_down[l])
            if l + 1 < depth:
                h, hn = post_pre(h, y, ffn2_norm_post[l], ffn1_norm_pre[l + 1], 0.5)
            else:
                h = post_pre(h, y, ffn2_norm_post[l], None, 0.5)
        outs.append(h)
    return jnp.stack(outs, axis=0)
```

```python
import functools
import math

import jax
import jax.numpy as jnp
from jax import lax
from jax.experimental import pallas as pl
from jax.experimental.pallas import tpu as pltpu

F32 = jnp.float32
BF16 = jnp.bfloat16

HEAD_DIM = 128
DIFF_QK_DIM = 64
ROT_DIM = 16
ROPE_THETA = 500000.0
NORM_EPS = 1e-6
XATTN_HEADS = 4
LANES = 128

VMEM_LIMIT_BYTES = 56 * 1024 * 1024
SB_EXIT = 100.0
LOG2_E = math.log2(math.e)


def _params(*sem):
    return pltpu.CompilerParams(dimension_semantics=sem, vmem_limit_bytes=VMEM_LIMIT_BYTES)


def _rms(x):
    return x * lax.rsqrt(jnp.mean(x * x, axis=-1, keepdims=True) + NORM_EPS)


def _prenorm_kernel(x_ref, g_ref, o_ref):
    o_ref[...] = (_rms(x_ref[...]) * g_ref[...]).astype(o_ref.dtype)


def prenorm(x, gain, *, tm=256):
    m, d = x.shape
    return pl.pallas_call(
        _prenorm_kernel,
        grid=(m // tm,),
        in_specs=[pl.BlockSpec((tm, d), lambda i: (i, 0)), pl.BlockSpec((1, d), lambda i: (0, 0))],
        out_specs=pl.BlockSpec((tm, d), lambda i: (i, 0)),
        out_shape=jax.ShapeDtypeStruct((m, d), BF16),
        compiler_params=_params("parallel"),
        name="prenorm",
    )(x, gain.reshape(1, d))


def _post_pre_kernel(h_ref, y_ref, gpost_ref, gpre_ref, hout_ref, hn_ref, *, coef):
    h = h_ref[...] + coef * (_rms(y_ref[...]) * gpost_ref[...])
    hout_ref[...] = h
    hn_ref[...] = (_rms(h) * gpre_ref[...]).astype(hn_ref.dtype)


def _post_kernel(h_ref, y_ref, gpost_ref, hout_ref, *, coef):
    hout_ref[...] = h_ref[...] + coef * (_rms(y_ref[...]) * gpost_ref[...])


def post_pre(h, y, gpost, gpre, coef, *, tm=256):
    m, d = h.shape
    row = pl.BlockSpec((tm, d), lambda i: (i, 0))
    vec = pl.BlockSpec((1, d), lambda i: (0, 0))
    if gpre is None:
        return pl.pallas_call(
            functools.partial(_post_kernel, coef=coef),
            grid=(m // tm,),
            in_specs=[row, row, vec],
            out_specs=row,
            out_shape=jax.ShapeDtypeStruct((m, d), F32),
            compiler_params=_params("parallel"),
            name="post_norm",
        )(h, y, gpost.reshape(1, d))
    return pl.pallas_call(
        functools.partial(_post_pre_kernel, coef=coef),
        grid=(m // tm,),
        in_specs=[row, row, vec, vec],
        out_specs=[row, row],
        out_shape=[jax.ShapeDtypeStruct((m, d), F32), jax.ShapeDtypeStruct((m, d), BF16)],
        compiler_params=_params("parallel"),
        name="post_pre_norm",
    )(h, y, gpost.reshape(1, d), gpre.reshape(1, d))


CAST_ROWS = 512


def _cast_weight_once(w_ref, w_bf):
    @pl.when(pl.program_id(1) == 0)
    def _():
        for r in range(0, w_ref.shape[0], CAST_ROWS):
            w_bf[r:r + CAST_ROWS, :] = w_ref[r:r + CAST_ROWS, :].astype(BF16)


def _ffn_up_kernel(x_ref, wg_hbm, wu_hbm, o_ref, w_even, w_odd, stage, sems, *, tn, n_tiles, n_rows):
    j, i = pl.program_id(0), pl.program_id(1)
    rows = w_even.shape[1] // n_rows
    w_hbm = (wg_hbm, wu_hbm)
    has_next = j + 1 < n_tiles

    def piece_copy(tile, piece, mat, slot):
        src = w_hbm[mat].at[pl.ds(pl.multiple_of(piece * rows, rows), rows), pl.ds(pl.multiple_of(tile * tn, tn), tn)]
        return pltpu.make_async_copy(src, stage.at[slot, mat], sems.at[slot, mat])

    def cast_piece(dst, piece, slot):
        for mat in range(2):
            dst[mat, pl.ds(pl.multiple_of(piece * rows, rows), rows), :] = stage[slot, mat].astype(BF16)

    @pl.when((j == 0) & (i == 0))
    def _():
        for piece in range(n_rows):
            for mat in range(2):
                piece_copy(0, piece, mat, 1).start()
            for mat in range(2):
                piece_copy(0, piece, mat, 1).wait()
            cast_piece(w_even, piece, 1)

    @pl.when(has_next & (i > 0))
    def _():
        for mat in range(2):
            piece_copy(j + 1, i - 1, mat, (i + 1) % 2).wait()

    @pl.when(has_next)
    def _():
        for mat in range(2):
            piece_copy(j + 1, i, mat, i % 2).start()

    def step(cur, nxt):
        cast_piece(nxt, (i + n_rows - 1) % n_rows, (i + 1) % 2)
        x = x_ref[...]
        g = jnp.dot(x, cur[0], preferred_element_type=F32)
        u = jnp.dot(x, cur[1], preferred_element_type=F32)
        o_ref[...] = (g * jax.nn.sigmoid(g) * u).astype(o_ref.dtype)

    pl.when(j % 2 == 0)(functools.partial(step, w_even, w_odd))
    pl.when(j % 2 == 1)(functools.partial(step, w_odd, w_even))

    last_slot = (n_rows - 1) % 2

    @pl.when(has_next & (i == n_rows - 1))
    def _():
        for mat in range(2):
            piece_copy(j + 1, n_rows - 1, mat, last_slot).wait()
        pl.when(j % 2 == 0)(functools.partial(cast_piece, w_odd, n_rows - 1, last_slot))
        pl.when(j % 2 == 1)(functools.partial(cast_piece, w_even, n_rows - 1, last_slot))


def ffn_up(xn, wg, wu, *, tm=1024, tn=512):
    m, k = xn.shape
    n = wg.shape[1]
    tm, tn = min(tm, m), min(tn, n)
    n_rows = m // tm
    rows = k // n_rows
    assert k % n_rows == 0 and rows % 16 == 0
    return pl.pallas_call(
        functools.partial(_ffn_up_kernel, tn=tn, n_tiles=n // tn, n_rows=n_rows),
        grid=(n // tn, n_rows),
        in_specs=[
            pl.BlockSpec((tm, k), lambda j, i: (i, 0)),
            pl.BlockSpec(memory_space=pl.ANY),
            pl.BlockSpec(memory_space=pl.ANY),
        ],
        out_specs=pl.BlockSpec((tm, tn), lambda j, i: (i, j)),
        out_shape=jax.ShapeDtypeStruct((m, n), BF16),
        scratch_shapes=[
            pltpu.VMEM((2, k, tn), BF16),
            pltpu.VMEM((2, k, tn), BF16),
            pltpu.VMEM((2, 2, rows, tn), F32),
            pltpu.SemaphoreType.DMA((2, 2)),
        ],
        compiler_params=_params("arbitrary", "arbitrary"),
        name="ffn_up",
    )(xn, wg, wu)


def _mm_acc_kernel(a_ref, b_ref, o_ref):
    @pl.when(pl.program_id(2) == 0)
    def _():
        o_ref[...] = jnp.zeros_like(o_ref)

    o_ref[...] += jnp.dot(a_ref[...], b_ref[...].astype(BF16), preferred_element_type=F32)


def matmul_kblocked(a, b, *, tm=2048, tn=1024, tk=1792):
    m, k = a.shape
    n = b.shape[1]
    tm, tn, tk = min(tm, m), min(tn, n), min(tk, k)
    while k % tk:
        tk -= LANES
    assert m % tm == 0 and n % tn == 0 and tk > 0
    return pl.pallas_call(
        _mm_acc_kernel,
        grid=(m // tm, n // tn, k // tk),
        in_specs=[
            pl.BlockSpec((tm, tk), lambda i, j, l: (i, l)),
            pl.BlockSpec((tk, tn), lambda i, j, l: (l, j)),
        ],
        out_specs=pl.BlockSpec((tm, tn), lambda i, j, l: (i, j)),
        out_shape=jax.ShapeDtypeStruct((m, n), F32),
        compiler_params=_params("parallel", "parallel", "arbitrary"),
        name="matmul_kblocked",
    )(a, b)


def _out_proj_kernel(a_ref, b_ref, wa_ref, wb_ref, o_ref):
    o_ref[...] = jnp.dot(a_ref[...], wa_ref[...], preferred_element_type=F32) + jnp.dot(
        b_ref[...], wb_ref[...], preferred_element_type=F32
    )


def out_proj(a, b, w, *, tm=1024, tn=1024):
    m, ka = a.shape
    kb = b.shape[1]
    n = w.shape[1]
    tm, tn = min(tm, m), min(tn, n)
    assert ka == kb
    return pl.pallas_call(
        _out_proj_kernel,
        grid=(m // tm, n // tn),
        in_specs=[
            pl.BlockSpec((tm, ka), lambda i, j: (i, 0)),
            pl.BlockSpec((tm, kb), lambda i, j: (i, 0)),
            pl.BlockSpec((ka, tn), lambda i, j: (0, j)),
            pl.BlockSpec((kb, tn), lambda i, j: (1, j)),
        ],
        out_specs=pl.BlockSpec((tm, tn), lambda i, j: (i, j)),
        out_shape=jax.ShapeDtypeStruct((m, n), F32),
        compiler_params=_params("parallel", "arbitrary"),
        name="out_proj",
    )(a, b, w, w)


def _rope_table_kernel(pos_ref, invf_ref, ct_ref, s1_ref, s2_ref):
    ang = pos_ref[...] * invf_ref[...]
    cos, sin = jnp.cos(ang), jnp.sin(ang)
    lane = lax.broadcasted_iota(jnp.int32, ang.shape, 1) % DIFF_QK_DIM
    half = ROT_DIM // 2
    ct_ref[...] = jnp.where(lane < ROT_DIM, cos, 1.0)
    s1_ref[...] = jnp.where(lane < half, -sin, 0.0)
    s2_ref[...] = jnp.where((lane >= half) & (lane < ROT_DIM), sin, 0.0)


def rope_tables(positions_col, *, tm=1024):
    s = positions_col.shape[0]
    tm = min(tm, s)
    half = ROT_DIM // 2
    inv_freq = ROPE_THETA ** (-jnp.arange(0, ROT_DIM, 2, dtype=F32) / ROT_DIM)
    invf = jnp.tile(inv_freq, LANES // half).reshape(1, LANES)
    tab = pl.BlockSpec((tm, LANES), lambda i: (i, 0))
    return pl.pallas_call(
        _rope_table_kernel,
        grid=(s // tm,),
        in_specs=[pl.BlockSpec((tm, 1), lambda i: (i, 0)), pl.BlockSpec((1, LANES), lambda i: (0, 0))],
        out_specs=[tab, tab, tab],
        out_shape=[jax.ShapeDtypeStruct((s, LANES), F32)] * 3,
        compiler_params=_params("parallel"),
        name="rope_tables",
    )(positions_col, invf)


def _in_proj_plain_kernel(x_ref, w_ref, o_ref, w_bf, *, scaled_tiles, scale):
    _cast_weight_once(w_ref, w_bf)
    j = pl.program_id(0)
    s = jnp.where((j >= scaled_tiles[0]) & (j < scaled_tiles[1]), scale, 1.0).astype(F32)
    o_ref[...] = (jnp.dot(x_ref[...], w_bf[...], preferred_element_type=F32) * s).astype(o_ref.dtype)


def _in_proj_rope_kernel(x_ref, w_ref, ct_ref, s1_ref, s2_ref, o_ref, w_bf, *, n_scaled_tiles, q_scale):
    _cast_weight_once(w_ref, w_bf)
    acc = jnp.dot(x_ref[...], w_bf[...], preferred_element_type=F32)
    scale = jnp.where(pl.program_id(0) < n_scaled_tiles, q_scale, 1.0).astype(F32)
    ct, s1, s2 = ct_ref[...], s1_ref[...], s2_ref[...]
    half = ROT_DIM // 2
    for c in range(acc.shape[1] // LANES):
        a = acc[:, c * LANES:(c + 1) * LANES]
        r = a * ct + pltpu.roll(a, LANES - half, 1) * s1 + pltpu.roll(a, half, 1) * s2
        o_ref[:, c * LANES:(c + 1) * LANES] = (r * scale).astype(o_ref.dtype)


def in_proj(xn, w, ct, s1, s2, *, rope_cols, scaled_cols, q_scale, sb_q_cols, sb_q_scale, tm=1024, tn=512):
    m, k = xn.shape
    n = w.shape[1]
    tm, tn = min(tm, m), min(tn, n, scaled_cols)
    assert rope_cols % tn == 0 and scaled_cols % tn == 0 and k % CAST_ROWS == 0
    assert all((c - rope_cols) % tn == 0 for c in sb_q_cols)
    sb_tiles = tuple((c - rope_cols) // tn for c in sb_q_cols)
    n_rope_tiles = rope_cols // tn
    x_spec = pl.BlockSpec((tm, k), lambda j, i: (i, 0))
    out_spec = pl.BlockSpec((tm, tn), lambda j, i: (i, j))
    tab = pl.BlockSpec((tm, LANES), lambda j, i: (i, 0))
    scratch = [pltpu.VMEM((k, tn), BF16)]
    roped = pl.pallas_call(
        functools.partial(_in_proj_rope_kernel, n_scaled_tiles=scaled_cols // tn, q_scale=q_scale),
        grid=(n_rope_tiles, m // tm),
        in_specs=[x_spec, pl.BlockSpec((k, tn), lambda j, i: (0, j)), tab, tab, tab],
        out_specs=out_spec,
        out_shape=jax.ShapeDtypeStruct((m, rope_cols), BF16),
        scratch_shapes=scratch,
        compiler_params=_params("arbitrary", "arbitrary"),
        name="in_proj_rope",
    )(xn, w, ct, s1, s2)
    plain = pl.pallas_call(
        functools.partial(_in_proj_plain_kernel, scaled_tiles=sb_tiles, scale=sb_q_scale),
        grid=((n - rope_cols) // tn, m // tm),
        in_specs=[x_spec, pl.BlockSpec((k, tn), lambda j, i: (0, j + n_rope_tiles))],
        out_specs=out_spec,
        out_shape=jax.ShapeDtypeStruct((m, n - rope_cols), BF16),
        scratch_shapes=scratch,
        compiler_params=_params("arbitrary", "arbitrary"),
        name="in_proj_plain",
    )(xn, w)
    return roped, plain


def _qk(q, k):
    return lax.dot_general(q, k, (((1,), (1,)), ((), ())), preferred_element_type=F32)


def _diff_attn_kernel(q_ref, k_ref, v_ref, lq1_ref, lk1_ref, lq2_ref, lk2_ref, g_ref, o_ref,
                      vp_ref, s_ref, p_ref, alpha_ref, m_ref, acc_ref, *, tq, lambda_init):
    tk = tq
    pair = pl.program_id(1)

    @pl.when(pair == 0)
    def _():
        vp_ref[:, :HEAD_DIM] = v_ref[...]
        vp_ref[:, HEAD_DIM:] = jnp.ones((v_ref.shape[0], HEAD_DIM), BF16)

    lane_q = lax.broadcasted_iota(jnp.int32, (tq, HEAD_DIM), 1)
    lane = lax.broadcasted_iota(jnp.int32, (2 * tq, LANES), 1)
    row = lax.broadcasted_iota(jnp.int32, (2 * tq, LANES), 0)
    row_in_tile = jnp.where(row >= tq, row - tq, row)

    class Tile:
        def __init__(self, t):
            self.t = t
            self.n_blocks = 2 * pair + t + 1
            q = q_ref[t * tq:(t + 1) * tq, :]
            zero = jnp.zeros_like(q)
            self.q_st = jnp.concatenate(
                [jnp.where(lane_q < DIFF_QK_DIM, q, zero), jnp.where(lane_q >= DIFF_QK_DIM, q, zero)], axis=0)
            m_ref[t] = jnp.full((2 * tq, LANES), -jnp.inf, F32)
            acc_ref[t] = jnp.zeros((2 * tq, 2 * HEAD_DIM), F32)

        def key_start(self, b):
            return pl.multiple_of(jnp.maximum(self.n_blocks - 1 - b, 0) * tk, tk)

        def scores(self, b, slot):
            s_ref[self.t, slot] = _qk(self.q_st, k_ref[pl.ds(self.key_start(b), tk), :])

        def softmax(self, slot, diagonal):
            t = self.t

            def chunk(c):
                s_c = s_ref[t, slot, :, c * LANES:(c + 1) * LANES]
                return jnp.where(lane + c * LANES <= row_in_tile, s_c, -jnp.inf) if diagonal else s_c

            m_blk = chunk(0)
            for c in range(1, tk // LANES):
                m_blk = jnp.maximum(m_blk, chunk(c))
            m_old = m_ref[t]
            m_new = jnp.maximum(m_old, jnp.max(m_blk, axis=-1, keepdims=True))
            alpha_ref[t, slot] = jnp.exp2(m_old - m_new)
            m_ref[t] = m_new
            for c in range(tk // LANES):
                p_ref[t, slot, :, c * LANES:(c + 1) * LANES] = jnp.exp2(chunk(c) - m_new).astype(BF16)

        def weighted_sum(self, b, slot):
            t = self.t
            pv = jnp.dot(p_ref[t, slot], vp_ref[pl.ds(self.key_start(b), tk), :],
                         preferred_element_type=F32)
            alpha = alpha_ref[t, slot]
            acc_ref[t, :, :HEAD_DIM] = alpha * acc_ref[t, :, :HEAD_DIM] + pv[:, :HEAD_DIM]
            acc_ref[t, :, HEAD_DIM:] = alpha * acc_ref[t, :, HEAD_DIM:] + pv[:, HEAD_DIM:]

        def finish(self, lam):
            acc = acc_ref[self.t]
            o1 = acc[:tq, :HEAD_DIM] / acc[:tq, HEAD_DIM:]
            o2 = acc[tq:, :HEAD_DIM] / acc[tq:, HEAD_DIM:]
            out = o1 - lam * o2
            o_ref[self.t * tq:(self.t + 1) * tq, :] = (
                (_rms(out) * g_ref[...]) * (1.0 - lambda_init)).astype(o_ref.dtype)

    tiles = (Tile(0), Tile(1))
    for tl in tiles:
        tl.scores(0, 0)
    for tl in tiles:
        tl.scores(1, 1)
    for tl in tiles:
        tl.softmax(0, True)

    def body(j, carry):
        b = 2 * j
        for tl in tiles:
            tl.weighted_sum(b, 0)
            tl.softmax(1, False)
            tl.scores(b + 2, 0)
        for tl in tiles:
            tl.weighted_sum(b + 1, 1)
            tl.softmax(0, False)
            tl.scores(b + 3, 1)
        return carry

    lax.fori_loop(0, pair, body, 0)
    last = 2 * pair
    tiles[1].softmax(1, False)
    for tl in tiles:
        tl.weighted_sum(last, 0)
    tiles[1].weighted_sum(last + 1, 1)

    lam = (jnp.exp(jnp.sum(lq1_ref[...] * lk1_ref[...], keepdims=True))
           - jnp.exp(jnp.sum(lq2_ref[...] * lk2_ref[...], keepdims=True)) + lambda_init)
    for tl in tiles:
        tl.finish(lam)


def diff_attention(qkv, lq1, lk1, lq2, lk2, subln, *, n_heads, q_col, k_col, v_col, lambda_init, tq=512):
    s = qkv[0].shape[0]
    tq = min(tq, s // 2)
    assert s % (2 * tq) == 0
    qb, kb, vb = q_col // HEAD_DIM, k_col // HEAD_DIM, v_col // HEAD_DIM
    lam_spec = pl.BlockSpec((1, DIFF_QK_DIM), lambda h, i: (0, 0))
    return pl.pallas_call(
        functools.partial(_diff_attn_kernel, tq=tq, lambda_init=lambda_init),
        grid=(n_heads, s // (2 * tq)),
        in_specs=[
            pl.BlockSpec((2 * tq, HEAD_DIM), lambda h, i: (i, qb + h)),
            pl.BlockSpec((s, HEAD_DIM), lambda h, i: (0, kb + h)),
            pl.BlockSpec((s, HEAD_DIM), lambda h, i: (0, vb + h)),
            lam_spec, lam_spec, lam_spec, lam_spec,
            pl.BlockSpec((1, HEAD_DIM), lambda h, i: (0, 0)),
        ],
        out_specs=pl.BlockSpec((2 * tq, HEAD_DIM), lambda h, i: (i, h)),
        out_shape=jax.ShapeDtypeStruct((s, n_heads * HEAD_DIM), BF16),
        scratch_shapes=[
            pltpu.VMEM((s, 2 * HEAD_DIM), BF16),
            pltpu.VMEM((2, 2, 2 * tq, tq), F32),
            pltpu.VMEM((2, 2, 2 * tq, tq), BF16),
            pltpu.VMEM((2, 2, 2 * tq, LANES), F32),
            pltpu.VMEM((2, 2 * tq, LANES), F32),
            pltpu.VMEM((2, 2 * tq, 2 * HEAD_DIM), F32),
        ],
        compiler_params=_params("arbitrary", "arbitrary"),
        name="diff_attention",
    )(*qkv, lq1.reshape(1, -1), lk1.reshape(1, -1), lq2.reshape(1, -1), lk2.reshape(1, -1), subln.reshape(1, -1))


def _sb_attn_kernel(q_ref, k_ref, v_ref, tri_ref, g_ref, o_ref, c_ref, acc_ref, *, tq, tk, heads):
    i = pl.program_id(1)
    tri = tri_ref[...]

    def block(h, kb, masked, c):
        cols = slice(h * HEAD_DIM, (h + 1) * HEAD_DIM)
        start = pl.multiple_of(kb * tk, tk)
        k = k_ref[pl.ds(start, tk), cols]
        v = v_ref[pl.ds(start, tk), cols]
        z = _qk(q_ref[:, cols], k)
        nz = -z
        lsm = jnp.minimum(nz, 0.0) - jnp.log2(1.0 + jnp.exp2(jnp.minimum(z, nz)))
        if masked:
            q_pos = i * tq + lax.broadcasted_iota(jnp.int32, z.shape, 0)
            k_pos = kb * tk + lax.broadcasted_iota(jnp.int32, z.shape, 1)
            strict = k_pos < q_pos
            lsm = jnp.where(strict, lsm, 0.0)
        hi = lsm.astype(BF16)
        lo = (lsm - hi.astype(F32)).astype(BF16)
        r = jnp.dot(hi, tri, preferred_element_type=F32) + jnp.dot(lo, tri, preferred_element_type=F32)
        a = jnp.exp2(z + r + c)
        if masked:
            a = jnp.where(strict, a, 0.0)
        return jnp.dot(a.astype(BF16), v, preferred_element_type=F32), c + r[:, 0:1]

    n_diag = tq // tk
    kb_top = (i + 1) * n_diag - 1

    def head_blocks(with_left):
        for h in range(heads):
            c = jnp.zeros((tq, 1), F32)
            acc = None
            for d in range(n_diag + int(with_left)):
                pv, c = block(h, kb_top - d, d < n_diag, c)
                acc = pv if acc is None else acc + pv
            acc_ref[h] = acc
            c_ref[h] = c

    pl.when(i == 0)(functools.partial(head_blocks, False))
    pl.when(i > 0)(functools.partial(head_blocks, True))

    def cond(carry):
        kb, cm = carry
        return jnp.logical_and(kb >= 0, cm > -SB_EXIT * LOG2_E)

    def body(carry):
        kb, _ = carry
        for h in range(heads):
            pv, c = block(h, kb, False, c_ref[h])
            acc_ref[h] += pv
            c_ref[h] = c
        return kb - 1, jnp.max(c_ref[...])

    lax.while_loop(cond, body, (kb_top - n_diag - 1, jnp.max(c_ref[...])))
    for h in range(heads):
        o_ref[:, h * HEAD_DIM:(h + 1) * HEAD_DIM] = (_rms(acc_ref[h]) * g_ref[...]).astype(o_ref.dtype)


def sb_attention(qkv, gain, *, n_heads, q_col, k_col, v_col, tq=256, tk=256, heads=4):
    s = qkv[0].shape[0]
    tq, tk, heads = min(tq, s), min(tk, s), min(heads, n_heads)
    width = heads * HEAD_DIM
    assert tq % tk == 0 and n_heads % heads == 0 and all(c % width == 0 for c in (q_col, k_col, v_col))
    qb, kb, vb = q_col // width, k_col // width, v_col // width
    tri = (lax.broadcasted_iota(jnp.int32, (tk, tk), 0) >= lax.broadcasted_iota(jnp.int32, (tk, tk), 1)).astype(BF16)
    return pl.pallas_call(
        functools.partial(_sb_attn_kernel, tq=tq, tk=tk, heads=heads),
        grid=(n_heads // heads, s // tq),
        in_specs=[
            pl.BlockSpec((tq, width), lambda g, i: (i, qb + g)),
            pl.BlockSpec((s, width), lambda g, i: (0, kb + g)),
            pl.BlockSpec((s, width), lambda g, i: (0, vb + g)),
            pl.BlockSpec((tk, tk), lambda g, i: (0, 0)),
            pl.BlockSpec((1, HEAD_DIM), lambda g, i: (0, 0)),
        ],
        out_specs=pl.BlockSpec((tq, width), lambda g, i: (i, g)),
        out_shape=jax.ShapeDtypeStruct((s, n_heads * HEAD_DIM), BF16),
        scratch_shapes=[pltpu.VMEM((heads, tq, 1), F32), pltpu.VMEM((heads, tq, HEAD_DIM), F32)],
        compiler_params=_params("parallel", "arbitrary"),
        name="sb_attention",
    )(*qkv, tri, gain.reshape(1, -1))


def _mem_kv_kernel(mem_ref, g_ref, w_ref, o_ref):
    mn = (_rms(mem_ref[...]) * g_ref[...]).astype(BF16)
    o_ref[...] = jnp.dot(mn, w_ref[...], preferred_element_type=F32).astype(o_ref.dtype)


def mem_kv(mem, gain, w_kv):
    mt, d = mem.shape
    n = w_kv.shape[1]
    return pl.pallas_call(
        _mem_kv_kernel,
        grid=(1,),
        in_specs=[pl.BlockSpec((mt, d), lambda i: (0, 0)), pl.BlockSpec((1, d), lambda i: (0, 0)),
                  pl.BlockSpec((d, n), lambda i: (0, 0))],
        out_specs=pl.BlockSpec((mt, n), lambda i: (0, 0)),
        out_shape=jax.ShapeDtypeStruct((mt, n), BF16),
        compiler_params=_params("arbitrary"),
        name="mem_kv",
    )(mem, gain.reshape(1, d), w_kv)


def _xattn_kernel(h_ref, y_ref, gmix_ref, gpre_ref, wq_ref, kv_ref, wo_ref, gpost_ref, gnext_ref, hout_ref, hn_ref, *,
                  scale):
    h = h_ref[...] + _rms(y_ref[...]) * gmix_ref[...]
    hn = (_rms(h) * gpre_ref[...]).astype(BF16)
    q = jnp.dot(hn, wq_ref[...], preferred_element_type=F32).astype(BF16)
    width = XATTN_HEADS * HEAD_DIM
    outs = []
    for hd in range(XATTN_HEADS):
        cols = slice(hd * HEAD_DIM, (hd + 1) * HEAD_DIM)
        k = kv_ref[:, cols]
        v = kv_ref[:, width + hd * HEAD_DIM:width + (hd + 1) * HEAD_DIM]
        sc = _qk(q[:, cols], k) * scale
        sc = sc - jnp.max(sc, axis=-1, keepdims=True)
        e = jnp.exp(sc)
        p = e / jnp.sum(e, axis=-1, keepdims=True)
        outs.append(jnp.dot(p.astype(BF16), v, preferred_element_type=F32).astype(BF16))
    o = jnp.concatenate(outs, axis=-1)
    y = jnp.dot(o, wo_ref[...], preferred_element_type=F32)
    h2 = h + _rms(y) * gpost_ref[...]
    hout_ref[...] = h2
    hn_ref[...] = (_rms(h2) * gnext_ref[...]).astype(hn_ref.dtype)


def xattn_block(h, y_mix, gmix, gpre, wq, kv, wo, gpost, gnext, *, tm=256):
    m, d = h.shape
    tm = min(tm, m)
    width = wq.shape[1]
    row = pl.BlockSpec((tm, d), lambda i: (i, 0))
    vec = pl.BlockSpec((1, d), lambda i: (0, 0))
    return pl.pallas_call(
        functools.partial(_xattn_kernel, scale=HEAD_DIM ** -0.5),
        grid=(m // tm,),
        in_specs=[row, row, vec, vec, pl.BlockSpec((d, width), lambda i: (0, 0)),
                  pl.BlockSpec(kv.shape, lambda i: (0, 0)), pl.BlockSpec((width, d), lambda i: (0, 0)), vec, vec],
        out_specs=[row, row],
        out_shape=[jax.ShapeDtypeStruct((m, d), F32), jax.ShapeDtypeStruct((m, d), BF16)],
        compiler_params=_params("parallel"),
        name="xattn_block",
    )(h, y_mix, gmix.reshape(1, d), gpre.reshape(1, d), wq, kv, wo, gpost.reshape(1, d), gnext.reshape(1, d))


def kernel(x, mem, positions, ffn1_norm_pre, ffn1_norm_post, ffn1_w_gate, ffn1_w_up, ffn1_w_down, mix_norm_pre, mix_norm_post, w_in, w_out, lambda_q1, lambda_k1, lambda_q2, lambda_k2, diff_subln, sb_norm, xattn_norm_pre, xattn_norm_post, mem_norm, xattn_w_q, xattn_w_kv, xattn_w_o, ffn2_norm_pre, ffn2_norm_post, ffn2_w_gate, ffn2_w_up, ffn2_w_down):
    b, s, d = x.shape
    depth = w_in.shape[0]
    in_width = w_in.shape[2]
    group = in_width // 6
    n_heads = group // HEAD_DIM
    outs = []
    for bi in range(b):
        h = x[bi]
        pos_col = positions[bi].astype(F32).reshape(s, 1)
        ct, s1, s2 = rope_tables(pos_col)
        hn = prenorm(h, ffn1_norm_pre[0])
        for l in range(depth):
            lambda_init = 0.8 - 0.6 * math.exp(-0.3 * l)
            cast = lambda w: w[l].astype(BF16)
            act = ffn_up(hn, ffn1_w_gate[l], ffn1_w_up[l])
            y = matmul_kblocked(act, ffn1_w_down[l])
            h, hn = post_pre(h, y, ffn1_norm_post[l], mix_norm_pre[l], 0.5)
            qk, rest = in_proj(hn, w_in[l], ct, s1, s2, rope_cols=2 * group, scaled_cols=group,
                               q_scale=DIFF_QK_DIM ** -0.5 * LOG2_E, sb_q_cols=(3 * group, 4 * group),
                               sb_q_scale=HEAD_DIM ** -0.5 * LOG2_E)
            diff_out = diff_attention((qk, qk, rest), lambda_q1[l], lambda_k1[l], lambda_q2[l], lambda_k2[l],
                                      diff_subln[l], n_heads=n_heads, q_col=0, k_col=group, v_col=0,
                                      lambda_init=lambda_init)
            sb_out = sb_attention((rest, rest, rest), sb_norm[l], n_heads=n_heads, q_col=group, k_col=2 * group,
                                  v_col=3 * group)
            y = out_proj(diff_out, sb_out, cast(w_out))
            kv = mem_kv(mem[bi], mem_norm[l], cast(xattn_w_kv))
            h, hn = xattn_block(h, y, mix_norm_post[l], xattn_norm_pre[l], cast(xattn_w_q), kv, cast(xattn_w_o),
                                xattn_norm_post[l], ffn2_norm_pre[l])
            act = ffn_up(hn, ffn2_w_gate[l], ffn2_w_up[l])
            y = matmul_kblocked(act, ffn2_w_down[l])
            if l + 1 < depth:
                h, hn = post_pre(h, y, ffn2_norm_post[l], ffn1_norm_pre[l + 1], 0.5)
            else:
                h = post_pre(h, y, ffn2_norm_post[l], None, 0.5)
        outs.append(h)
    return jnp.stack(outs, axis=0)
```

```python
import functools
import math

import jax
import jax.numpy as jnp
from jax import lax
from jax.experimental import pallas as pl
from jax.experimental.pallas import tpu as pltpu

F32 = jnp.float32
BF16 = jnp.bfloat16

HEAD_DIM = 128
DIFF_QK_DIM = 64
ROT_DIM = 16
ROPE_THETA = 500000.0
NORM_EPS = 1e-6
XATTN_HEADS = 4
LANES = 128

VMEM_LIMIT_BYTES = 56 * 1024 * 1024
SB_EXIT = 100.0
LOG2_E = math.log2(math.e)


def _params(*sem, flags=None):
    return pltpu.CompilerParams(dimension_semantics=sem, vmem_limit_bytes=VMEM_LIMIT_BYTES, flags=flags)


def _rms(x):
    return x * lax.rsqrt(jnp.mean(x * x, axis=-1, keepdims=True) + NORM_EPS)


def _prenorm_kernel(x_ref, g_ref, o_ref):
    o_ref[...] = (_rms(x_ref[...]) * g_ref[...]).astype(o_ref.dtype)


def prenorm(x, gain, *, tm=256):
    m, d = x.shape
    return pl.pallas_call(
        _prenorm_kernel,
        grid=(m // tm,),
        in_specs=[pl.BlockSpec((tm, d), lambda i: (i, 0)), pl.BlockSpec((1, d), lambda i: (0, 0))],
        out_specs=pl.BlockSpec((tm, d), lambda i: (i, 0)),
        out_shape=jax.ShapeDtypeStruct((m, d), BF16),
        compiler_params=_params("parallel"),
        name="prenorm",
    )(x, gain.reshape(1, d))


def _post_pre_kernel(h_ref, y_ref, gpost_ref, gpre_ref, hout_ref, hn_ref, *, coef):
    h = h_ref[...] + coef * (_rms(y_ref[...]) * gpost_ref[...])
    hout_ref[...] = h
    hn_ref[...] = (_rms(h) * gpre_ref[...]).astype(hn_ref.dtype)


def _post_kernel(h_ref, y_ref, gpost_ref, hout_ref, *, coef):
    hout_ref[...] = h_ref[...] + coef * (_rms(y_ref[...]) * gpost_ref[...])


def post_pre(h, y, gpost, gpre, coef, *, tm=256):
    m, d = h.shape
    row = pl.BlockSpec((tm, d), lambda i: (i, 0))
    vec = pl.BlockSpec((1, d), lambda i: (0, 0))
    if gpre is None:
        return pl.pallas_call(
            functools.partial(_post_kernel, coef=coef),
            grid=(m // tm,),
            in_specs=[row, row, vec],
            out_specs=row,
            out_shape=jax.ShapeDtypeStruct((m, d), F32),
            compiler_params=_params("parallel"),
            name="post_norm",
        )(h, y, gpost.reshape(1, d))
    return pl.pallas_call(
        functools.partial(_post_pre_kernel, coef=coef),
        grid=(m // tm,),
        in_specs=[row, row, vec, vec],
        out_specs=[row, row],
        out_shape=[jax.ShapeDtypeStruct((m, d), F32), jax.ShapeDtypeStruct((m, d), BF16)],
        compiler_params=_params("parallel"),
        name="post_pre_norm",
    )(h, y, gpost.reshape(1, d), gpre.reshape(1, d))


CAST_ROWS = 512


def _cast_weight_once(w_ref, w_bf):
    @pl.when(pl.program_id(1) == 0)
    def _():
        for r in range(0, w_ref.shape[0], CAST_ROWS):
            w_bf[r:r + CAST_ROWS, :] = w_ref[r:r + CAST_ROWS, :].astype(BF16)


def _ffn_up_kernel(x_ref, wg_hbm, wu_hbm, o_ref, w_even, w_odd, stage, sems, *, tn, n_tiles, n_rows):
    j, i = pl.program_id(0), pl.program_id(1)
    rows = w_even.shape[1] // n_rows
    w_hbm = (wg_hbm, wu_hbm)
    has_next = j + 1 < n_tiles

    def piece_copy(tile, piece, mat, slot):
        src = w_hbm[mat].at[pl.ds(pl.multiple_of(piece * rows, rows), rows), pl.ds(pl.multiple_of(tile * tn, tn), tn)]
        return pltpu.make_async_copy(src, stage.at[slot, mat], sems.at[slot, mat])

    def cast_piece(dst, piece, slot):
        for mat in range(2):
            dst[mat, pl.ds(pl.multiple_of(piece * rows, rows), rows), :] = stage[slot, mat].astype(BF16)

    @pl.when((j == 0) & (i == 0))
    def _():
        for piece in range(n_rows):
            for mat in range(2):
                piece_copy(0, piece, mat, 1).start()
            for mat in range(2):
                piece_copy(0, piece, mat, 1).wait()
            cast_piece(w_even, piece, 1)

    @pl.when(has_next & (i > 0))
    def _():
        for mat in range(2):
            piece_copy(j + 1, i - 1, mat, (i + 1) % 2).wait()

    @pl.when(has_next)
    def _():
        for mat in range(2):
            piece_copy(j + 1, i, mat, i % 2).start()

    def step(cur, nxt):
        cast_piece(nxt, (i + n_rows - 1) % n_rows, (i + 1) % 2)
        x = x_ref[...]
        g = jnp.dot(x, cur[0], preferred_element_type=F32)
        u = jnp.dot(x, cur[1], preferred_element_type=F32)
        o_ref[...] = (g * jax.nn.sigmoid(g) * u).astype(o_ref.dtype)

    pl.when(j % 2 == 0)(functools.partial(step, w_even, w_odd))
    pl.when(j % 2 == 1)(functools.partial(step, w_odd, w_even))

    last_slot = (n_rows - 1) % 2

    @pl.when(has_next & (i == n_rows - 1))
    def _():
        for mat in range(2):
            piece_copy(j + 1, n_rows - 1, mat, last_slot).wait()
        pl.when(j % 2 == 0)(functools.partial(cast_piece, w_odd, n_rows - 1, last_slot))
        pl.when(j % 2 == 1)(functools.partial(cast_piece, w_even, n_rows - 1, last_slot))


def ffn_up(xn, wg, wu, *, tm=1024, tn=512):
    m, k = xn.shape
    n = wg.shape[1]
    tm, tn = min(tm, m), min(tn, n)
    n_rows = m // tm
    rows = k // n_rows
    assert k % n_rows == 0 and rows % 16 == 0
    return pl.pallas_call(
        functools.partial(_ffn_up_kernel, tn=tn, n_tiles=n // tn, n_rows=n_rows),
        grid=(n // tn, n_rows),
        in_specs=[
            pl.BlockSpec((tm, k), lambda j, i: (i, 0)),
            pl.BlockSpec(memory_space=pl.ANY),
            pl.BlockSpec(memory_space=pl.ANY),
        ],
        out_specs=pl.BlockSpec((tm, tn), lambda j, i: (i, j)),
        out_shape=jax.ShapeDtypeStruct((m, n), BF16),
        scratch_shapes=[
            pltpu.VMEM((2, k, tn), BF16),
            pltpu.VMEM((2, k, tn), BF16),
            pltpu.VMEM((2, 2, rows, tn), F32),
            pltpu.SemaphoreType.DMA((2, 2)),
        ],
        compiler_params=_params("arbitrary", "arbitrary"),
        name="ffn_up",
    )(xn, wg, wu)


def _mm_acc_kernel(a_ref, b_ref, o_ref):
    @pl.when(pl.program_id(2) == 0)
    def _():
        o_ref[...] = jnp.zeros_like(o_ref)

    o_ref[...] += jnp.dot(a_ref[...], b_ref[...].astype(BF16), preferred_element_type=F32)


def matmul_kblocked(a, b, *, tm=2048, tn=1024, tk=1792):
    m, k = a.shape
    n = b.shape[1]
    tm, tn, tk = min(tm, m), min(tn, n), min(tk, k)
    while k % tk:
        tk -= LANES
    assert m % tm == 0 and n % tn == 0 and tk > 0
    return pl.pallas_call(
        _mm_acc_kernel,
        grid=(m // tm, n // tn, k // tk),
        in_specs=[
            pl.BlockSpec((tm, tk), lambda i, j, l: (i, l)),
            pl.BlockSpec((tk, tn), lambda i, j, l: (l, j)),
        ],
        out_specs=pl.BlockSpec((tm, tn), lambda i, j, l: (i, j)),
        out_shape=jax.ShapeDtypeStruct((m, n), F32),
        compiler_params=_params("parallel", "parallel", "arbitrary"),
        name="matmul_kblocked",
    )(a, b)


def _out_proj_kernel(a_ref, b_ref, wa_ref, wb_ref, o_ref):
    o_ref[...] = jnp.dot(a_ref[...], wa_ref[...], preferred_element_type=F32) + jnp.dot(
        b_ref[...], wb_ref[...], preferred_element_type=F32
    )


def out_proj(a, b, w, *, tm=1024, tn=1024):
    m, ka = a.shape
    kb = b.shape[1]
    n = w.shape[1]
    tm, tn = min(tm, m), min(tn, n)
    assert ka == kb
    return pl.pallas_call(
        _out_proj_kernel,
        grid=(m // tm, n // tn),
        in_specs=[
            pl.BlockSpec((tm, ka), lambda i, j: (i, 0)),
            pl.BlockSpec((tm, kb), lambda i, j: (i, 0)),
            pl.BlockSpec((ka, tn), lambda i, j: (0, j)),
            pl.BlockSpec((kb, tn), lambda i, j: (1, j)),
        ],
        out_specs=pl.BlockSpec((tm, tn), lambda i, j: (i, j)),
        out_shape=jax.ShapeDtypeStruct((m, n), F32),
        compiler_params=_params("parallel", "arbitrary"),
        name="out_proj",
    )(a, b, w, w)


def _rope_table_kernel(pos_ref, invf_ref, ct_ref, s1_ref, s2_ref):
    ang = pos_ref[...] * invf_ref[...]
    cos, sin = jnp.cos(ang), jnp.sin(ang)
    lane = lax.broadcasted_iota(jnp.int32, ang.shape, 1) % DIFF_QK_DIM
    half = ROT_DIM // 2
    ct_ref[...] = jnp.where(lane < ROT_DIM, cos, 1.0)
    s1_ref[...] = jnp.where(lane < half, -sin, 0.0)
    s2_ref[...] = jnp.where((lane >= half) & (lane < ROT_DIM), sin, 0.0)


def rope_tables(positions_col, *, tm=1024):
    s = positions_col.shape[0]
    tm = min(tm, s)
    half = ROT_DIM // 2
    inv_freq = ROPE_THETA ** (-jnp.arange(0, ROT_DIM, 2, dtype=F32) / ROT_DIM)
    invf = jnp.tile(inv_freq, LANES // half).reshape(1, LANES)
    tab = pl.BlockSpec((tm, LANES), lambda i: (i, 0))
    return pl.pallas_call(
        _rope_table_kernel,
        grid=(s // tm,),
        in_specs=[pl.BlockSpec((tm, 1), lambda i: (i, 0)), pl.BlockSpec((1, LANES), lambda i: (0, 0))],
        out_specs=[tab, tab, tab],
        out_shape=[jax.ShapeDtypeStruct((s, LANES), F32)] * 3,
        compiler_params=_params("parallel"),
        name="rope_tables",
    )(positions_col, invf)


def _in_proj_plain_kernel(x_ref, w_ref, o_ref, w_bf, *, scaled_tiles, scale):
    _cast_weight_once(w_ref, w_bf)
    j = pl.program_id(0)
    s = jnp.where((j >= scaled_tiles[0]) & (j < scaled_tiles[1]), scale, 1.0).astype(F32)
    o_ref[...] = (jnp.dot(x_ref[...], w_bf[...], preferred_element_type=F32) * s).astype(o_ref.dtype)


def _in_proj_rope_kernel(x_ref, w_ref, ct_ref, s1_ref, s2_ref, o_ref, w_bf, *, n_scaled_tiles, q_scale):
    _cast_weight_once(w_ref, w_bf)
    scale = jnp.where(pl.program_id(0) < n_scaled_tiles, q_scale, 1.0).astype(F32)
    half = ROT_DIM // 2
    hm = x_ref.shape[0] // 2
    for rows in (slice(0, hm), slice(hm, 2 * hm)):
        acc = jnp.dot(x_ref[rows, :], w_bf[...], preferred_element_type=F32)
        ct, s1, s2 = ct_ref[rows, :], s1_ref[rows, :], s2_ref[rows, :]
        for c in range(acc.shape[1] // LANES):
            a = acc[:, c * LANES:(c + 1) * LANES]
            r = a * ct + pltpu.roll(a, LANES - half, 1) * s1 + pltpu.roll(a, half, 1) * s2
            o_ref[rows, c * LANES:(c + 1) * LANES] = (r * scale).astype(o_ref.dtype)


def in_proj(xn, w, ct, s1, s2, *, rope_cols, scaled_cols, q_scale, sb_q_cols, sb_q_scale, tm=1024, tn=512):
    m, k = xn.shape
    n = w.shape[1]
    tm, tn = min(tm, m), min(tn, n, scaled_cols)
    assert rope_cols % tn == 0 and scaled_cols % tn == 0 and k % CAST_ROWS == 0
    assert all((c - rope_cols) % tn == 0 for c in sb_q_cols)
    sb_tiles = tuple((c - rope_cols) // tn for c in sb_q_cols)
    n_rope_tiles = rope_cols // tn
    x_spec = pl.BlockSpec((tm, k), lambda j, i: (i, 0))
    out_spec = pl.BlockSpec((tm, tn), lambda j, i: (i, j))
    tab = pl.BlockSpec((tm, LANES), lambda j, i: (i, 0))
    scratch = [pltpu.VMEM((k, tn), BF16)]
    roped = pl.pallas_call(
        functools.partial(_in_proj_rope_kernel, n_scaled_tiles=scaled_cols // tn, q_scale=q_scale),
        grid=(n_rope_tiles, m // tm),
        in_specs=[x_spec, pl.BlockSpec((k, tn), lambda j, i: (0, j)), tab, tab, tab],
        out_specs=out_spec,
        out_shape=jax.ShapeDtypeStruct((m, rope_cols), BF16),
        scratch_shapes=scratch,
        compiler_params=_params("arbitrary", "arbitrary"),
        name="in_proj_rope",
    )(xn, w, ct, s1, s2)
    plain = pl.pallas_call(
        functools.partial(_in_proj_plain_kernel, scaled_tiles=sb_tiles, scale=sb_q_scale),
        grid=((n - rope_cols) // tn, m // tm),
        in_specs=[x_spec, pl.BlockSpec((k, tn), lambda j, i: (0, j + n_rope_tiles))],
        out_specs=out_spec,
        out_shape=jax.ShapeDtypeStruct((m, n - rope_cols), BF16),
        scratch_shapes=scratch,
        compiler_params=_params("arbitrary", "arbitrary"),
        name="in_proj_plain",
    )(xn, w)
    return roped, plain


def _qk(q, k):
    return lax.dot_general(q, k, (((1,), (1,)), ((), ())), preferred_element_type=F32)


def _diff_attn_kernel(q_ref, k_ref, v_ref, lq1_ref, lk1_ref, lq2_ref, lk2_ref, g_ref, o_ref,
                      vp_ref, s_ref, m_ref, acc_ref, *, tq, lambda_init):
    tk = tq
    pair = pl.program_id(1)

    @pl.when(pair == 0)
    def _():
        vp_ref[:, :HEAD_DIM] = v_ref[...]
        vp_ref[:, HEAD_DIM:] = jnp.ones((v_ref.shape[0], HEAD_DIM), BF16)

    lane_q = lax.broadcasted_iota(jnp.int32, (tq, HEAD_DIM), 1)
    lane = lax.broadcasted_iota(jnp.int32, (2 * tq, LANES), 1)
    row = lax.broadcasted_iota(jnp.int32, (2 * tq, LANES), 0)
    row_in_tile = jnp.where(row >= tq, row - tq, row)

    class Tile:
        def __init__(self, t):
            self.t = t
            self.n_blocks = 2 * pair + t + 1
            q = q_ref[t * tq:(t + 1) * tq, :]
            zero = jnp.zeros_like(q)
            self.q_st = jnp.concatenate(
                [jnp.where(lane_q < DIFF_QK_DIM, q, zero), jnp.where(lane_q >= DIFF_QK_DIM, q, zero)], axis=0)
            m_ref[t] = jnp.full((2 * tq, LANES), -jnp.inf, F32)
            acc_ref[t] = jnp.zeros((2 * tq, 2 * HEAD_DIM), F32)

        def key_start(self, b):
            return pl.multiple_of(jnp.maximum(self.n_blocks - 1 - b, 0) * tk, tk)

        def scores(self, b, slot):
            s_ref[self.t, slot] = _qk(self.q_st, k_ref[pl.ds(self.key_start(b), tk), :])

        def softmax_sum(self, b, slot, diagonal):
            t = self.t

            def chunk(c):
                s_c = s_ref[t, slot, :, c * LANES:(c + 1) * LANES]
                return jnp.where(lane + c * LANES <= row_in_tile, s_c, -jnp.inf) if diagonal else s_c

            m_blk = chunk(0)
            for c in range(1, tk // LANES):
                m_blk = jnp.maximum(m_blk, chunk(c))
            m_old = m_ref[t]
            m_new = jnp.maximum(m_old, jnp.max(m_blk, axis=-1, keepdims=True))
            alpha = jnp.exp2(m_old - m_new)
            m_ref[t] = m_new
            p = jnp.concatenate([jnp.exp2(chunk(c) - m_new).astype(BF16) for c in range(tk // LANES)], axis=1)
            pv = jnp.dot(p, vp_ref[pl.ds(self.key_start(b), tk), :], preferred_element_type=F32)
            acc_ref[t, :, :HEAD_DIM] = alpha * acc_ref[t, :, :HEAD_DIM] + pv[:, :HEAD_DIM]
            acc_ref[t, :, HEAD_DIM:] = alpha * acc_ref[t, :, HEAD_DIM:] + pv[:, HEAD_DIM:]

        def finish(self, lam):
            acc = acc_ref[self.t]
            o1 = acc[:tq, :HEAD_DIM] / acc[:tq, HEAD_DIM:]
            o2 = acc[tq:, :HEAD_DIM] / acc[tq:, HEAD_DIM:]
            out = o1 - lam * o2
            o_ref[self.t * tq:(self.t + 1) * tq, :] = (
                (_rms(out) * g_ref[...]) * (1.0 - lambda_init)).astype(o_ref.dtype)

    tiles = (Tile(0), Tile(1))
    for tl in tiles:
        tl.scores(0, 0)
    for tl in tiles:
        tl.scores(1, 1)
    for tl in tiles:
        tl.softmax_sum(0, 0, True)

    def body(j, carry):
        b = 2 * j + 1
        for tl in tiles:
            tl.scores(b + 1, 0)
            tl.softmax_sum(b, 1, False)
        for tl in tiles:
            tl.scores(b + 2, 1)
            tl.softmax_sum(b + 1, 0, False)
        return carry

    lax.fori_loop(0, pair, body, 0)
    tiles[1].softmax_sum(2 * pair + 1, 1, False)

    lam = (jnp.exp(jnp.sum(lq1_ref[...] * lk1_ref[...], keepdims=True))
           - jnp.exp(jnp.sum(lq2_ref[...] * lk2_ref[...], keepdims=True)) + lambda_init)
    for tl in tiles:
        tl.finish(lam)


def diff_attention(qkv, lq1, lk1, lq2, lk2, subln, *, n_heads, q_col, k_col, v_col, lambda_init, tq=512):
    s = qkv[0].shape[0]
    tq = min(tq, s // 2)
    assert s % (2 * tq) == 0
    qb, kb, vb = q_col // HEAD_DIM, k_col // HEAD_DIM, v_col // HEAD_DIM
    lam_spec = pl.BlockSpec((1, DIFF_QK_DIM), lambda h, i: (0, 0))
    return pl.pallas_call(
        functools.partial(_diff_attn_kernel, tq=tq, lambda_init=lambda_init),
        grid=(n_heads, s // (2 * tq)),
        in_specs=[
            pl.BlockSpec((2 * tq, HEAD_DIM), lambda h, i: (i, qb + h)),
            pl.BlockSpec((s, HEAD_DIM), lambda h, i: (0, kb + h)),
            pl.BlockSpec((s, HEAD_DIM), lambda h, i: (0, vb + h)),
            lam_spec, lam_spec, lam_spec, lam_spec,
            pl.BlockSpec((1, HEAD_DIM), lambda h, i: (0, 0)),
        ],
        out_specs=pl.BlockSpec((2 * tq, HEAD_DIM), lambda h, i: (i, h)),
        out_shape=jax.ShapeDtypeStruct((s, n_heads * HEAD_DIM), BF16),
        scratch_shapes=[
            pltpu.VMEM((s, 2 * HEAD_DIM), BF16),
            pltpu.VMEM((2, 2, 2 * tq, tq), F32),
            pltpu.VMEM((2, 2 * tq, LANES), F32),
            pltpu.VMEM((2, 2 * tq, 2 * HEAD_DIM), F32),
        ],
        compiler_params=_params("arbitrary", "arbitrary"),
        name="diff_attention",
    )(*qkv, lq1.reshape(1, -1), lk1.reshape(1, -1), lq2.reshape(1, -1), lk2.reshape(1, -1), subln.reshape(1, -1))


def _sb_attn_kernel(q_ref, k_ref, v_ref, tri_ref, g_ref, o_ref, c_ref, acc_ref, *, tq, tk, heads):
    i = pl.program_id(1)
    tri = tri_ref[...]

    def block(h, kb, masked, c):
        cols = slice(h * HEAD_DIM, (h + 1) * HEAD_DIM)
        start = pl.multiple_of(kb * tk, tk)
        k = k_ref[pl.ds(start, tk), cols]
        v = v_ref[pl.ds(start, tk), cols]
        z = _qk(q_ref[:, cols], k)
        nz = -z
        lsm = jnp.minimum(nz, 0.0) - jnp.log2(1.0 + jnp.exp2(jnp.minimum(z, nz)))
        if masked:
            q_pos = i * tq + lax.broadcasted_iota(jnp.int32, z.shape, 0)
            k_pos = kb * tk + lax.broadcasted_iota(jnp.int32, z.shape, 1)
            strict = k_pos < q_pos
            lsm = jnp.where(strict, lsm, 0.0)
        hi = lsm.astype(BF16)
        lo = (lsm - hi.astype(F32)).astype(BF16)
        r = jnp.dot(hi, tri, preferred_element_type=F32) + jnp.dot(lo, tri, preferred_element_type=F32)
        a = jnp.exp2(z + r + c)
        if masked:
            a = jnp.where(strict, a, 0.0)
        return jnp.dot(a.astype(BF16), v, preferred_element_type=F32), c + r[:, 0:1]

    n_diag = tq // tk
    kb_top = (i + 1) * n_diag - 1

    def head_blocks(with_left):
        for h in range(heads):
            c = jnp.zeros((tq, 1), F32)
            acc = None
            for d in range(n_diag + int(with_left)):
                pv, c = block(h, kb_top - d, d < n_diag, c)
                acc = pv if acc is None else acc + pv
            acc_ref[h] = acc
            c_ref[h] = c

    pl.when(i == 0)(functools.partial(head_blocks, False))
    pl.when(i > 0)(functools.partial(head_blocks, True))

    def cond(carry):
        kb, cm = carry
        return jnp.logical_and(kb >= 0, cm > -SB_EXIT * LOG2_E)

    def body(carry):
        kb, _ = carry
        for h in range(heads):
            pv, c = block(h, kb, False, c_ref[h])
            acc_ref[h] += pv
            c_ref[h] = c
        return kb - 1, jnp.max(c_ref[...])

    lax.while_loop(cond, body, (kb_top - n_diag - 1, jnp.max(c_ref[...])))
    for h in range(heads):
        o_ref[:, h * HEAD_DIM:(h + 1) * HEAD_DIM] = (_rms(acc_ref[h]) * g_ref[...]).astype(o_ref.dtype)


def sb_attention(qkv, gain, *, n_heads, q_col, k_col, v_col, tq=256, tk=256, heads=4):
    s = qkv[0].shape[0]
    tq, tk, heads = min(tq, s), min(tk, s), min(heads, n_heads)
    width = heads * HEAD_DIM
    assert tq % tk == 0 and n_heads % heads == 0 and all(c % width == 0 for c in (q_col, k_col, v_col))
    qb, kb, vb = q_col // width, k_col // width, v_col // width
    tri = (lax.broadcasted_iota(jnp.int32, (tk, tk), 0) >= lax.broadcasted_iota(jnp.int32, (tk, tk), 1)).astype(BF16)
    return pl.pallas_call(
        functools.partial(_sb_attn_kernel, tq=tq, tk=tk, heads=heads),
        grid=(n_heads // heads, s // tq),
        in_specs=[
            pl.BlockSpec((tq, width), lambda g, i: (i, qb + g)),
            pl.BlockSpec((s, width), lambda g, i: (0, kb + g)),
            pl.BlockSpec((s, width), lambda g, i: (0, vb + g)),
            pl.BlockSpec((tk, tk), lambda g, i: (0, 0)),
            pl.BlockSpec((1, HEAD_DIM), lambda g, i: (0, 0)),
        ],
        out_specs=pl.BlockSpec((tq, width), lambda g, i: (i, g)),
        out_shape=jax.ShapeDtypeStruct((s, n_heads * HEAD_DIM), BF16),
        scratch_shapes=[pltpu.VMEM((heads, tq, 1), F32), pltpu.VMEM((heads, tq, HEAD_DIM), F32)],
        compiler_params=_params("parallel", "arbitrary"),
        name="sb_attention",
    )(*qkv, tri, gain.reshape(1, -1))


def _mem_kv_kernel(mem_ref, g_ref, w_ref, o_ref):
    mn = (_rms(mem_ref[...]) * g_ref[...]).astype(BF16)
    o_ref[...] = jnp.dot(mn, w_ref[...], preferred_element_type=F32).astype(o_ref.dtype)


def mem_kv(mem, gain, w_kv):
    mt, d = mem.shape
    n = w_kv.shape[1]
    return pl.pallas_call(
        _mem_kv_kernel,
        grid=(1,),
        in_specs=[pl.BlockSpec((mt, d), lambda i: (0, 0)), pl.BlockSpec((1, d), lambda i: (0, 0)),
                  pl.BlockSpec((d, n), lambda i: (0, 0))],
        out_specs=pl.BlockSpec((mt, n), lambda i: (0, 0)),
        out_shape=jax.ShapeDtypeStruct((mt, n), BF16),
        compiler_params=_params("arbitrary"),
        name="mem_kv",
    )(mem, gain.reshape(1, d), w_kv)


def _xattn_kernel(h_ref, y_ref, gmix_ref, gpre_ref, wq_ref, kv_ref, wo_ref, gpost_ref, gnext_ref, hout_ref, hn_ref, *,
                  scale):
    h = h_ref[...] + _rms(y_ref[...]) * gmix_ref[...]
    hn = (_rms(h) * gpre_ref[...]).astype(BF16)
    q = jnp.dot(hn, wq_ref[...], preferred_element_type=F32).astype(BF16)
    width = XATTN_HEADS * HEAD_DIM
    outs = []
    for hd in range(XATTN_HEADS):
        cols = slice(hd * HEAD_DIM, (hd + 1) * HEAD_DIM)
        k = kv_ref[:, cols]
        v = kv_ref[:, width + hd * HEAD_DIM:width + (hd + 1) * HEAD_DIM]
        sc = _qk(q[:, cols], k) * scale
        sc = sc - jnp.max(sc, axis=-1, keepdims=True)
        e = jnp.exp(sc)
        p = e / jnp.sum(e, axis=-1, keepdims=True)
        outs.append(jnp.dot(p.astype(BF16), v, preferred_element_type=F32).astype(BF16))
    o = jnp.concatenate(outs, axis=-1)
    y = jnp.dot(o, wo_ref[...], preferred_element_type=F32)
    h2 = h + _rms(y) * gpost_ref[...]
    hout_ref[...] = h2
    hn_ref[...] = (_rms(h2) * gnext_ref[...]).astype(hn_ref.dtype)


def xattn_block(h, y_mix, gmix, gpre, wq, kv, wo, gpost, gnext, *, tm=256):
    m, d = h.shape
    tm = min(tm, m)
    width = wq.shape[1]
    row = pl.BlockSpec((tm, d), lambda i: (i, 0))
    vec = pl.BlockSpec((1, d), lambda i: (0, 0))
    return pl.pallas_call(
        functools.partial(_xattn_kernel, scale=HEAD_DIM ** -0.5),
        grid=(m // tm,),
        in_specs=[row, row, vec, vec, pl.BlockSpec((d, width), lambda i: (0, 0)),
                  pl.BlockSpec(kv.shape, lambda i: (0, 0)), pl.BlockSpec((width, d), lambda i: (0, 0)), vec, vec],
        out_specs=[row, row],
        out_shape=[jax.ShapeDtypeStruct((m, d), F32), jax.ShapeDtypeStruct((m, d), BF16)],
        compiler_params=_params("parallel"),
        name="xattn_block",
    )(h, y_mix, gmix.reshape(1, d), gpre.reshape(1, d), wq, kv, wo, gpost.reshape(1, d), gnext.reshape(1, d))


def kernel(x, mem, positions, ffn1_norm_pre, ffn1_norm_post, ffn1_w_gate, ffn1_w_up, ffn1_w_down, mix_norm_pre, mix_norm_post, w_in, w_out, lambda_q1, lambda_k1, lambda_q2, lambda_k2, diff_subln, sb_norm, xattn_norm_pre, xattn_norm_post, mem_norm, xattn_w_q, xattn_w_kv, xattn_w_o, ffn2_norm_pre, ffn2_norm_post, ffn2_w_gate, ffn2_w_up, ffn2_w_down):
    b, s, d = x.shape
    depth = w_in.shape[0]
    in_width = w_in.shape[2]
    group = in_width // 6
    n_heads = group // HEAD_DIM
    outs = []
    for bi in range(b):
        h = x[bi]
        pos_col = positions[bi].astype(F32).reshape(s, 1)
        ct, s1, s2 = rope_tables(pos_col)
        hn = prenorm(h, ffn1_norm_pre[0])
        for l in range(depth):
            lambda_init = 0.8 - 0.6 * math.exp(-0.3 * l)
            cast = lambda w: w[l].astype(BF16)
            act = ffn_up(hn, ffn1_w_gate[l], ffn1_w_up[l])
            y = matmul_kblocked(act, ffn1_w_down[l])
            h, hn = post_pre(h, y, ffn1_norm_post[l], mix_norm_pre[l], 0.5)
            qk, rest = in_proj(hn, w_in[l], ct, s1, s2, rope_cols=2 * group, scaled_cols=group,
                               q_scale=DIFF_QK_DIM ** -0.5 * LOG2_E, sb_q_cols=(3 * group, 4 * group),
                               sb_q_scale=HEAD_DIM ** -0.5 * LOG2_E)
            diff_out = diff_attention((qk, qk, rest), lambda_q1[l], lambda_k1[l], lambda_q2[l], lambda_k2[l],
                                      diff_subln[l], n_heads=n_heads, q_col=0, k_col=group, v_col=0,
                                      lambda_init=lambda_init)
            sb_out = sb_attention((rest, rest, rest), sb_norm[l], n_heads=n_heads, q_col=group, k_col=2 * group,
                                  v_col=3 * group)
            y = out_proj(diff_out, sb_out, cast(w_out))
            kv = mem_kv(mem[bi], mem_norm[l], cast(xattn_w_kv))
            h, hn = xattn_block(h, y, mix_norm_post[l], xattn_norm_pre[l], cast(xattn_w_q), kv, cast(xattn_w_o),
                                xattn_norm_post[l], ffn2_norm_pre[l])
            act = ffn_up(hn, ffn2_w_gate[l], ffn2_w_up[l])
            y = matmul_kblocked(act, ffn2_w_down[l])
            if l + 1 < depth:
                h, hn = post_pre(h, y, ffn2_norm_post[l], ffn1_norm_pre[l + 1], 0.5)
            else:
                h = post_pre(h, y, ffn2_norm_post[l], None, 0.5)
        outs.append(h)
    return jnp.stack(outs, axis=0)
```

```python
import functools
import math

import jax
import jax.numpy as jnp
from jax import lax
from jax.experimental import pallas as pl
from jax.experimental.pallas import tpu as pltpu

F32 = jnp.float32
BF16 = jnp.bfloat16

HEAD_DIM = 128
DIFF_QK_DIM = 64
ROT_DIM = 16
ROPE_THETA = 500000.0
NORM_EPS = 1e-6
XATTN_HEADS = 4
LANES = 128

VMEM_LIMIT_BYTES = 56 * 1024 * 1024
SB_EXIT = 100.0
LOG2_E = math.log2(math.e)


def _params(*sem):
    return pltpu.CompilerParams(dimension_semantics=sem, vmem_limit_bytes=VMEM_LIMIT_BYTES)


def _rms(x):
    return x * lax.rsqrt(jnp.mean(x * x, axis=-1, keepdims=True) + NORM_EPS)


def _prenorm_kernel(x_ref, g_ref, o_ref):
    o_ref[...] = (_rms(x_ref[...]) * g_ref[...]).astype(o_ref.dtype)


def prenorm(x, gain, *, tm=256):
    m, d = x.shape
    return pl.pallas_call(
        _prenorm_kernel,
        grid=(m // tm,),
        in_specs=[pl.BlockSpec((tm, d), lambda i: (i, 0)), pl.BlockSpec((1, d), lambda i: (0, 0))],
        out_specs=pl.BlockSpec((tm, d), lambda i: (i, 0)),
        out_shape=jax.ShapeDtypeStruct((m, d), BF16),
        compiler_params=_params("parallel"),
        name="prenorm",
    )(x, gain.reshape(1, d))


def _post_pre_kernel(h_ref, y_ref, gpost_ref, gpre_ref, hout_ref, hn_ref, *, coef):
    h = h_ref[...] + coef * (_rms(y_ref[...]) * gpost_ref[...])
    hout_ref[...] = h
    hn_ref[...] = (_rms(h) * gpre_ref[...]).astype(hn_ref.dtype)


def _post_kernel(h_ref, y_ref, gpost_ref, hout_ref, *, coef):
    hout_ref[...] = h_ref[...] + coef * (_rms(y_ref[...]) * gpost_ref[...])


def post_pre(h, y, gpost, gpre, coef, *, tm=256):
    m, d = h.shape
    row = pl.BlockSpec((tm, d), lambda i: (i, 0))
    vec = pl.BlockSpec((1, d), lambda i: (0, 0))
    if gpre is None:
        return pl.pallas_call(
            functools.partial(_post_kernel, coef=coef),
            grid=(m // tm,),
            in_specs=[row, row, vec],
            out_specs=row,
            out_shape=jax.ShapeDtypeStruct((m, d), F32),
            compiler_params=_params("parallel"),
            name="post_norm",
        )(h, y, gpost.reshape(1, d))
    return pl.pallas_call(
        functools.partial(_post_pre_kernel, coef=coef),
        grid=(m // tm,),
        in_specs=[row, row, vec, vec],
        out_specs=[row, row],
        out_shape=[jax.ShapeDtypeStruct((m, d), F32), jax.ShapeDtypeStruct((m, d), BF16)],
        compiler_params=_params("parallel"),
        name="post_pre_norm",
    )(h, y, gpost.reshape(1, d), gpre.reshape(1, d))


CAST_ROWS = 512


def _cast_weight_once(w_ref, w_bf):
    @pl.when(pl.program_id(1) == 0)
    def _():
        for r in range(0, w_ref.shape[0], CAST_ROWS):
            w_bf[r:r + CAST_ROWS, :] = w_ref[r:r + CAST_ROWS, :].astype(BF16)


def _ffn_up_kernel(x_ref, wg_hbm, wu_hbm, o_ref, w_even, w_odd, stage, sems, *, tn, n_tiles, n_rows):
    j, i = pl.program_id(0), pl.program_id(1)
    rows = w_even.shape[1] // n_rows
    w_hbm = (wg_hbm, wu_hbm)
    has_next = j + 1 < n_tiles

    def piece_copy(tile, piece, mat, slot):
        src = w_hbm[mat].at[pl.ds(pl.multiple_of(piece * rows, rows), rows), pl.ds(pl.multiple_of(tile * tn, tn), tn)]
        return pltpu.make_async_copy(src, stage.at[slot, mat], sems.at[slot, mat])

    def cast_piece(dst, piece, slot):
        for mat in range(2):
            dst[mat, pl.ds(pl.multiple_of(piece * rows, rows), rows), :] = stage[slot, mat].astype(BF16)

    @pl.when((j == 0) & (i == 0))
    def _():
        for piece in range(n_rows):
            for mat in range(2):
                piece_copy(0, piece, mat, 1).start()
            for mat in range(2):
                piece_copy(0, piece, mat, 1).wait()
            cast_piece(w_even, piece, 1)

    @pl.when(has_next & (i > 0))
    def _():
        for mat in range(2):
            piece_copy(j + 1, i - 1, mat, (i + 1) % 2).wait()

    @pl.when(has_next)
    def _():
        for mat in range(2):
            piece_copy(j + 1, i, mat, i % 2).start()

    def step(cur, nxt):
        cast_piece(nxt, (i + n_rows - 1) % n_rows, (i + 1) % 2)
        x = x_ref[...]
        g = jnp.dot(x, cur[0], preferred_element_type=F32)
        u = jnp.dot(x, cur[1], preferred_element_type=F32)
        o_ref[...] = (g * jax.nn.sigmoid(g) * u).astype(o_ref.dtype)

    pl.when(j % 2 == 0)(functools.partial(step, w_even, w_odd))
    pl.when(j % 2 == 1)(functools.partial(step, w_odd, w_even))

    last_slot = (n_rows - 1) % 2

    @pl.when(has_next & (i == n_rows - 1))
    def _():
        for mat in range(2):
            piece_copy(j + 1, n_rows - 1, mat, last_slot).wait()
        pl.when(j % 2 == 0)(functools.partial(cast_piece, w_odd, n_rows - 1, last_slot))
        pl.when(j % 2 == 1)(functools.partial(cast_piece, w_even, n_rows - 1, last_slot))


def ffn_up(xn, wg, wu, *, tm=1024, tn=512):
    m, k = xn.shape
    n = wg.shape[1]
    tm, tn = min(tm, m), min(tn, n)
    n_rows = m // tm
    rows = k // n_rows
    assert k % n_rows == 0 and rows % 16 == 0
    return pl.pallas_call(
        functools.partial(_ffn_up_kernel, tn=tn, n_tiles=n // tn, n_rows=n_rows),
        grid=(n // tn, n_rows),
        in_specs=[
            pl.BlockSpec((tm, k), lambda j, i: (i, 0)),
            pl.BlockSpec(memory_space=pl.ANY),
            pl.BlockSpec(memory_space=pl.ANY),
        ],
        out_specs=pl.BlockSpec((tm, tn), lambda j, i: (i, j)),
        out_shape=jax.ShapeDtypeStruct((m, n), BF16),
        scratch_shapes=[
            pltpu.VMEM((2, k, tn), BF16),
            pltpu.VMEM((2, k, tn), BF16),
            pltpu.VMEM((2, 2, rows, tn), F32),
            pltpu.SemaphoreType.DMA((2, 2)),
        ],
        compiler_params=_params("arbitrary", "arbitrary"),
        name="ffn_up",
    )(xn, wg, wu)


def _mm_acc_kernel(a_ref, b_ref, o_ref):
    @pl.when(pl.program_id(2) == 0)
    def _():
        o_ref[...] = jnp.zeros_like(o_ref)

    o_ref[...] += jnp.dot(a_ref[...], b_ref[...].astype(BF16), preferred_element_type=F32)


def matmul_kblocked(a, b, *, tm=2048, tn=1024, tk=1792):
    m, k = a.shape
    n = b.shape[1]
    tm, tn, tk = min(tm, m), min(tn, n), min(tk, k)
    while k % tk:
        tk -= LANES
    assert m % tm == 0 and n % tn == 0 and tk > 0
    return pl.pallas_call(
        _mm_acc_kernel,
        grid=(m // tm, n // tn, k // tk),
        in_specs=[
            pl.BlockSpec((tm, tk), lambda i, j, l: (i, l)),
            pl.BlockSpec((tk, tn), lambda i, j, l: (l, j)),
        ],
        out_specs=pl.BlockSpec((tm, tn), lambda i, j, l: (i, j)),
        out_shape=jax.ShapeDtypeStruct((m, n), F32),
        compiler_params=_params("parallel", "parallel", "arbitrary"),
        name="matmul_kblocked",
    )(a, b)


def _out_proj_kernel(a_ref, b_ref, wa_ref, wb_ref, o_ref):
    o_ref[...] = jnp.dot(a_ref[...], wa_ref[...], preferred_element_type=F32) + jnp.dot(
        b_ref[...], wb_ref[...], preferred_element_type=F32
    )


def out_proj(a, b, w, *, tm=1024, tn=1024):
    m, ka = a.shape
    kb = b.shape[1]
    n = w.shape[1]
    tm, tn = min(tm, m), min(tn, n)
    assert ka == kb
    return pl.pallas_call(
        _out_proj_kernel,
        grid=(m // tm, n // tn),
        in_specs=[
            pl.BlockSpec((tm, ka), lambda i, j: (i, 0)),
            pl.BlockSpec((tm, kb), lambda i, j: (i, 0)),
            pl.BlockSpec((ka, tn), lambda i, j: (0, j)),
            pl.BlockSpec((kb, tn), lambda i, j: (1, j)),
        ],
        out_specs=pl.BlockSpec((tm, tn), lambda i, j: (i, j)),
        out_shape=jax.ShapeDtypeStruct((m, n), F32),
        compiler_params=_params("parallel", "arbitrary"),
        name="out_proj",
    )(a, b, w, w)


def _rope_table_kernel(pos_ref, invf_ref, ct_ref, s1_ref, s2_ref):
    ang = pos_ref[...] * invf_ref[...]
    cos, sin = jnp.cos(ang), jnp.sin(ang)
    lane = lax.broadcasted_iota(jnp.int32, ang.shape, 1) % DIFF_QK_DIM
    half = ROT_DIM // 2
    ct_ref[...] = jnp.where(lane < ROT_DIM, cos, 1.0)
    s1_ref[...] = jnp.where(lane < half, -sin, 0.0)
    s2_ref[...] = jnp.where((lane >= half) & (lane < ROT_DIM), sin, 0.0)


def rope_tables(positions_col, *, tm=1024):
    s = positions_col.shape[0]
    tm = min(tm, s)
    half = ROT_DIM // 2
    inv_freq = ROPE_THETA ** (-jnp.arange(0, ROT_DIM, 2, dtype=F32) / ROT_DIM)
    invf = jnp.tile(inv_freq, LANES // half).reshape(1, LANES)
    tab = pl.BlockSpec((tm, LANES), lambda i: (i, 0))
    return pl.pallas_call(
        _rope_table_kernel,
        grid=(s // tm,),
        in_specs=[pl.BlockSpec((tm, 1), lambda i: (i, 0)), pl.BlockSpec((1, LANES), lambda i: (0, 0))],
        out_specs=[tab, tab, tab],
        out_shape=[jax.ShapeDtypeStruct((s, LANES), F32)] * 3,
        compiler_params=_params("parallel"),
        name="rope_tables",
    )(positions_col, invf)


def _in_proj_plain_kernel(x_ref, w_ref, o_ref, w_bf, *, scaled_tiles, scale):
    _cast_weight_once(w_ref, w_bf)
    j = pl.program_id(0)
    s = jnp.where((j >= scaled_tiles[0]) & (j < scaled_tiles[1]), scale, 1.0).astype(F32)
    o_ref[...] = (jnp.dot(x_ref[...], w_bf[...], preferred_element_type=F32) * s).astype(o_ref.dtype)


def _in_proj_rope_kernel(x_ref, w_ref, ct_ref, s1_ref, s2_ref, o_ref, w_bf, *, n_scaled_tiles, q_scale):
    _cast_weight_once(w_ref, w_bf)
    scale = jnp.where(pl.program_id(0) < n_scaled_tiles, q_scale, 1.0).astype(F32)
    half = ROT_DIM // 2
    hm = x_ref.shape[0] // 2
    for rows in (slice(0, hm), slice(hm, 2 * hm)):
        acc = jnp.dot(x_ref[rows, :], w_bf[...], preferred_element_type=F32)
        ct, s1, s2 = ct_ref[rows, :], s1_ref[rows, :], s2_ref[rows, :]
        for c in range(acc.shape[1] // LANES):
            a = acc[:, c * LANES:(c + 1) * LANES]
            r = a * ct + pltpu.roll(a, LANES - half, 1) * s1 + pltpu.roll(a, half, 1) * s2
            o_ref[rows, c * LANES:(c + 1) * LANES] = (r * scale).astype(o_ref.dtype)


def in_proj(xn, w, ct, s1, s2, *, rope_cols, scaled_cols, q_scale, sb_q_cols, sb_q_scale, tm=1024, tn=512):
    m, k = xn.shape
    n = w.shape[1]
    tm, tn = min(tm, m), min(tn, n, scaled_cols)
    assert rope_cols % tn == 0 and scaled_cols % tn == 0 and k % CAST_ROWS == 0
    assert all((c - rope_cols) % tn == 0 for c in sb_q_cols)
    sb_tiles = tuple((c - rope_cols) // tn for c in sb_q_cols)
    n_rope_tiles = rope_cols // tn
    x_spec = pl.BlockSpec((tm, k), lambda j, i: (i, 0))
    out_spec = pl.BlockSpec((tm, tn), lambda j, i: (i, j))
    tab = pl.BlockSpec((tm, LANES), lambda j, i: (i, 0))
    scratch = [pltpu.VMEM((k, tn), BF16)]
    roped = pl.pallas_call(
        functools.partial(_in_proj_rope_kernel, n_scaled_tiles=scaled_cols // tn, q_scale=q_scale),
        grid=(n_rope_tiles, m // tm),
        in_specs=[x_spec, pl.BlockSpec((k, tn), lambda j, i: (0, j)), tab, tab, tab],
        out_specs=out_spec,
        out_shape=jax.ShapeDtypeStruct((m, rope_cols), BF16),
        scratch_shapes=scratch,
        compiler_params=_params("arbitrary", "arbitrary"),
        name="in_proj_rope",
    )(xn, w, ct, s1, s2)
    plain = pl.pallas_call(
        functools.partial(_in_proj_plain_kernel, scaled_tiles=sb_tiles, scale=sb_q_scale),
        grid=((n - rope_cols) // tn, m // tm),
        in_specs=[x_spec, pl.BlockSpec((k, tn), lambda j, i: (0, j + n_rope_tiles))],
        out_specs=out_spec,
        out_shape=jax.ShapeDtypeStruct((m, n - rope_cols), BF16),
        scratch_shapes=scratch,
        compiler_params=_params("arbitrary", "arbitrary"),
        name="in_proj_plain",
    )(xn, w)
    return roped, plain


def _qk(q, k):
    return lax.dot_general(q, k, (((1,), (1,)), ((), ())), preferred_element_type=F32)


def _diff_attn_kernel(q_ref, k_ref, v_ref, lq1_ref, lk1_ref, lq2_ref, lk2_ref, g_ref, o_ref,
                      vp_ref, s_ref, m_ref, acc_ref, *, tq, lambda_init):
    tk = tq
    pair = pl.program_id(1)

    @pl.when(pair == 0)
    def _():
        vp_ref[:, :HEAD_DIM] = v_ref[...]
        vp_ref[:, HEAD_DIM:] = jnp.ones((v_ref.shape[0], HEAD_DIM), BF16)

    lane_q = lax.broadcasted_iota(jnp.int32, (tq, HEAD_DIM), 1)
    lane = lax.broadcasted_iota(jnp.int32, (2 * tq, LANES), 1)
    row = lax.broadcasted_iota(jnp.int32, (2 * tq, LANES), 0)
    row_in_tile = jnp.where(row >= tq, row - tq, row)

    class Tile:
        def __init__(self, t):
            self.t = t
            self.n_blocks = 2 * pair + t + 1
            q = q_ref[t * tq:(t + 1) * tq, :]
            zero = jnp.zeros_like(q)
            self.q_st = jnp.concatenate(
                [jnp.where(lane_q < DIFF_QK_DIM, q, zero), jnp.where(lane_q >= DIFF_QK_DIM, q, zero)], axis=0)
            m_ref[t] = jnp.full((2 * tq, LANES), -jnp.inf, F32)
            acc_ref[t] = jnp.zeros((2 * tq, 2 * HEAD_DIM), F32)

        def key_start(self, b):
            return pl.multiple_of(jnp.maximum(self.n_blocks - 1 - b, 0) * tk, tk)

        def scores(self, b, slot):
            s_ref[self.t, slot] = _qk(self.q_st, k_ref[pl.ds(self.key_start(b), tk), :])

        def softmax_sum(self, b, slot, diagonal):
            t = self.t

            def chunk(c):
                s_c = s_ref[t, slot, :, c * LANES:(c + 1) * LANES]
                return jnp.where(lane + c * LANES <= row_in_tile, s_c, -jnp.inf) if diagonal else s_c

            m_blk = chunk(0)
            for c in range(1, tk // LANES):
                m_blk = jnp.maximum(m_blk, chunk(c))
            m_old = m_ref[t]
            m_new = jnp.maximum(m_old, jnp.max(m_blk, axis=-1, keepdims=True))
            alpha = jnp.exp2(m_old - m_new)
            m_ref[t] = m_new
            p = jnp.concatenate([jnp.exp2(chunk(c) - m_new).astype(BF16) for c in range(tk // LANES)], axis=1)
            pv = jnp.dot(p, vp_ref[pl.ds(self.key_start(b), tk), :], preferred_element_type=F32)
            acc_ref[t, :, :HEAD_DIM] = alpha * acc_ref[t, :, :HEAD_DIM] + pv[:, :HEAD_DIM]
            acc_ref[t, :, HEAD_DIM:] = alpha * acc_ref[t, :, HEAD_DIM:] + pv[:, HEAD_DIM:]

        def finish(self, lam):
            acc = acc_ref[self.t]
            o1 = acc[:tq, :HEAD_DIM] / acc[:tq, HEAD_DIM:]
            o2 = acc[tq:, :HEAD_DIM] / acc[tq:, HEAD_DIM:]
            out = o1 - lam * o2
            o_ref[self.t * tq:(self.t + 1) * tq, :] = (
                (_rms(out) * g_ref[...]) * (1.0 - lambda_init)).astype(o_ref.dtype)

    tiles = (Tile(0), Tile(1))
    for tl in tiles:
        tl.scores(0, 0)
    for tl in tiles:
        tl.scores(1, 1)
    for tl in tiles:
        tl.softmax_sum(0, 0, True)

    def body(j, carry):
        b = 2 * j + 1
        for tl in tiles:
            tl.scores(b + 1, 0)
            tl.softmax_sum(b, 1, False)
        for tl in tiles:
            tl.scores(b + 2, 1)
            tl.softmax_sum(b + 1, 0, False)
        return carry

    lax.fori_loop(0, pair, body, 0)
    tiles[1].softmax_sum(2 * pair + 1, 1, False)

    lam = (jnp.exp(jnp.sum(lq1_ref[...] * lk1_ref[...], keepdims=True))
           - jnp.exp(jnp.sum(lq2_ref[...] * lk2_ref[...], keepdims=True)) + lambda_init)
    for tl in tiles:
        tl.finish(lam)


def diff_attention(qkv, lq1, lk1, lq2, lk2, subln, *, n_heads, q_col, k_col, v_col, lambda_init, tq=512):
    s = qkv[0].shape[0]
    tq = min(tq, s // 2)
    assert s % (2 * tq) == 0
    qb, kb, vb = q_col // HEAD_DIM, k_col // HEAD_DIM, v_col // HEAD_DIM
    lam_spec = pl.BlockSpec((1, DIFF_QK_DIM), lambda h, i: (0, 0))
    return pl.pallas_call(
        functools.partial(_diff_attn_kernel, tq=tq, lambda_init=lambda_init),
        grid=(n_heads, s // (2 * tq)),
        in_specs=[
            pl.BlockSpec((2 * tq, HEAD_DIM), lambda h, i: (i, qb + h)),
            pl.BlockSpec((s, HEAD_DIM), lambda h, i: (0, kb + h)),
            pl.BlockSpec((s, HEAD_DIM), lambda h, i: (0, vb + h)),
            lam_spec, lam_spec, lam_spec, lam_spec,
            pl.BlockSpec((1, HEAD_DIM), lambda h, i: (0, 0)),
        ],
        out_specs=pl.BlockSpec((2 * tq, HEAD_DIM), lambda h, i: (i, h)),
        out_shape=jax.ShapeDtypeStruct((s, n_heads * HEAD_DIM), BF16),
        scratch_shapes=[
            pltpu.VMEM((s, 2 * HEAD_DIM), BF16),
            pltpu.VMEM((2, 2, 2 * tq, tq), F32),
            pltpu.VMEM((2, 2 * tq, LANES), F32),
            pltpu.VMEM((2, 2 * tq, 2 * HEAD_DIM), F32),
        ],
        compiler_params=_params("arbitrary", "arbitrary"),
        name="diff_attention",
    )(*qkv, lq1.reshape(1, -1), lk1.reshape(1, -1), lq2.reshape(1, -1), lk2.reshape(1, -1), subln.reshape(1, -1))


def _sb_attn_kernel(q_ref, k_ref, v_ref, tri_ref, g_ref, o_ref, c_ref, acc_ref, *, tq, tk, heads):
    i = pl.program_id(1)
    tri = tri_ref[...]

    def block(h, kb, masked, c):
        cols = slice(h * HEAD_DIM, (h + 1) * HEAD_DIM)
        start = pl.multiple_of(kb * tk, tk)
        k = k_ref[pl.ds(start, tk), cols]
        v = v_ref[pl.ds(start, tk), cols]
        z = _qk(q_ref[:, cols], k)
        nz = -z
        lsm = jnp.minimum(nz, 0.0) - jnp.log2(1.0 + jnp.exp2(jnp.minimum(z, nz)))
        if masked:
            q_pos = i * tq + lax.broadcasted_iota(jnp.int32, z.shape, 0)
            k_pos = kb * tk + lax.broadcasted_iota(jnp.int32, z.shape, 1)
            strict = k_pos < q_pos
            lsm = jnp.where(strict, lsm, 0.0)
        hi = lsm.astype(BF16)
        lo = (lsm - hi.astype(F32)).astype(BF16)
        r = jnp.dot(hi, tri, preferred_element_type=F32) + jnp.dot(lo, tri, preferred_element_type=F32)
        a = jnp.exp2(z + r + c)
        if masked:
            a = jnp.where(strict, a, 0.0)
        return jnp.dot(a.astype(BF16), v, preferred_element_type=F32), c + r[:, 0:1]

    n_diag = tq // tk
    kb_top = (i + 1) * n_diag - 1

    def head_blocks(with_left):
        for h in range(heads):
            c = jnp.zeros((tq, 1), F32)
            acc = None
            for d in range(n_diag + int(with_left)):
                pv, c = block(h, kb_top - d, d < n_diag, c)
                acc = pv if acc is None else acc + pv
            acc_ref[h] = acc
            c_ref[h] = c

    pl.when(i == 0)(functools.partial(head_blocks, False))
    pl.when(i > 0)(functools.partial(head_blocks, True))

    def cond(carry):
        kb, cm = carry
        return jnp.logical_and(kb >= 0, cm > -SB_EXIT * LOG2_E)

    def body(carry):
        kb, _ = carry
        for h in range(heads):
            pv, c = block(h, kb, False, c_ref[h])
            acc_ref[h] += pv
            c_ref[h] = c
        return kb - 1, jnp.max(c_ref[...])

    lax.while_loop(cond, body, (kb_top - n_diag - 1, jnp.max(c_ref[...])))
    for h in range(heads):
        o_ref[:, h * HEAD_DIM:(h + 1) * HEAD_DIM] = (_rms(acc_ref[h]) * g_ref[...]).astype(o_ref.dtype)


def sb_attention(qkv, gain, *, n_heads, q_col, k_col, v_col, tq=256, tk=256, heads=4):
    s = qkv[0].shape[0]
    tq, tk, heads = min(tq, s), min(tk, s), min(heads, n_heads)
    width = heads * HEAD_DIM
    assert tq % tk == 0 and n_heads % heads == 0 and all(c % width == 0 for c in (q_col, k_col, v_col))
    qb, kb, vb = q_col // width, k_col // width, v_col // width
    tri = (lax.broadcasted_iota(jnp.int32, (tk, tk), 0) >= lax.broadcasted_iota(jnp.int32, (tk, tk), 1)).astype(BF16)
    return pl.pallas_call(
        functools.partial(_sb_attn_kernel, tq=tq, tk=tk, heads=heads),
        grid=(n_heads // heads, s // tq),
        in_specs=[
            pl.BlockSpec((tq, width), lambda g, i: (i, qb + g)),
            pl.BlockSpec((s, width), lambda g, i: (0, kb + g)),
            pl.BlockSpec((s, width), lambda g, i: (0, vb + g)),
            pl.BlockSpec((tk, tk), lambda g, i: (0, 0)),
            pl.BlockSpec((1, HEAD_DIM), lambda g, i: (0, 0)),
        ],
        out_specs=pl.BlockSpec((tq, width), lambda g, i: (i, g)),
        out_shape=jax.ShapeDtypeStruct((s, n_heads * HEAD_DIM), BF16),
        scratch_shapes=[pltpu.VMEM((heads, tq, 1), F32), pltpu.VMEM((heads, tq, HEAD_DIM), F32)],
        compiler_params=_params("parallel", "arbitrary"),
        name="sb_attention",
    )(*qkv, tri, gain.reshape(1, -1))


def _mem_kv_kernel(mem_ref, g_ref, w_ref, o_ref):
    mn = (_rms(mem_ref[...]) * g_ref[...]).astype(BF16)
    o_ref[...] = jnp.dot(mn, w_ref[...], preferred_element_type=F32).astype(o_ref.dtype)


def mem_kv(mem, gain, w_kv):
    mt, d = mem.shape
    n = w_kv.shape[1]
    return pl.pallas_call(
        _mem_kv_kernel,
        grid=(1,),
        in_specs=[pl.BlockSpec((mt, d), lambda i: (0, 0)), pl.BlockSpec((1, d), lambda i: (0, 0)),
                  pl.BlockSpec((d, n), lambda i: (0, 0))],
        out_specs=pl.BlockSpec((mt, n), lambda i: (0, 0)),
        out_shape=jax.ShapeDtypeStruct((mt, n), BF16),
        compiler_params=_params("arbitrary"),
        name="mem_kv",
    )(mem, gain.reshape(1, d), w_kv)


def _xattn_kernel(h_ref, y_ref, gmix_ref, gpre_ref, wq_ref, kv_ref, wo_ref, gpost_ref, gnext_ref, hout_ref, hn_ref, *,
                  scale):
    h = h_ref[...] + _rms(y_ref[...]) * gmix_ref[...]
    hn = (_rms(h) * gpre_ref[...]).astype(BF16)
    q = jnp.dot(hn, wq_ref[...], preferred_element_type=F32).astype(BF16)
    width = XATTN_HEADS * HEAD_DIM
    outs = []
    for hd in range(XATTN_HEADS):
        cols = slice(hd * HEAD_DIM, (hd + 1) * HEAD_DIM)
        k = kv_ref[:, cols]
        v = kv_ref[:, width + hd * HEAD_DIM:width + (hd + 1) * HEAD_DIM]
        sc = _qk(q[:, cols], k) * scale
        sc = sc - jnp.max(sc, axis=-1, keepdims=True)
        e = jnp.exp(sc)
        p = e / jnp.sum(e, axis=-1, keepdims=True)
        outs.append(jnp.dot(p.astype(BF16), v, preferred_element_type=F32).astype(BF16))
    o = jnp.concatenate(outs, axis=-1)
    y = jnp.dot(o, wo_ref[...], preferred_element_type=F32)
    h2 = h + _rms(y) * gpost_ref[...]
    hout_ref[...] = h2
    hn_ref[...] = (_rms(h2) * gnext_ref[...]).astype(hn_ref.dtype)


def xattn_block(h, y_mix, gmix, gpre, wq, kv, wo, gpost, gnext, *, tm=256):
    m, d = h.shape
    tm = min(tm, m)
    width = wq.shape[1]
    row = pl.BlockSpec((tm, d), lambda i: (i, 0))
    vec = pl.BlockSpec((1, d), lambda i: (0, 0))
    return pl.pallas_call(
        functools.partial(_xattn_kernel, scale=HEAD_DIM ** -0.5),
        grid=(m // tm,),
        in_specs=[row, row, vec, vec, pl.BlockSpec((d, width), lambda i: (0, 0)),
                  pl.BlockSpec(kv.shape, lambda i: (0, 0)), pl.BlockSpec((width, d), lambda i: (0, 0)), vec, vec],
        out_specs=[row, row],
        out_shape=[jax.ShapeDtypeStruct((m, d), F32), jax.ShapeDtypeStruct((m, d), BF16)],
        compiler_params=_params("parallel"),
        name="xattn_block",
    )(h, y_mix, gmix.reshape(1, d), gpre.reshape(1, d), wq, kv, wo, gpost.reshape(1, d), gnext.reshape(1, d))


def kernel(x, mem, positions, ffn1_norm_pre, ffn1_norm_post, ffn1_w_gate, ffn1_w_up, ffn1_w_down, mix_norm_pre, mix_norm_post, w_in, w_out, lambda_q1, lambda_k1, lambda_q2, lambda_k2, diff_subln, sb_norm, xattn_norm_pre, xattn_norm_post, mem_norm, xattn_w_q, xattn_w_kv, xattn_w_o, ffn2_norm_pre, ffn2_norm_post, ffn2_w_gate, ffn2_w_up, ffn2_w_down):
    b, s, d = x.shape
    depth = w_in.shape[0]
    in_width = w_in.shape[2]
    group = in_width // 6
    n_heads = group // HEAD_DIM
    outs = []
    for bi in range(b):
        h = x[bi]
        pos_col = positions[bi].astype(F32).reshape(s, 1)
        ct, s1, s2 = rope_tables(pos_col)
        hn = prenorm(h, ffn1_norm_pre[0])
        for l in range(depth):
            lambda_init = 0.8 - 0.6 * math.exp(-0.3 * l)
            cast = lambda w: w[l].astype(BF16)
            act = ffn_up(hn, ffn1_w_gate[l], ffn1_w_up[l])
            y = matmul_kblocked(act, ffn1_w_down[l])
            h, hn = post_pre(h, y, ffn1_norm_post[l], mix_norm_pre[l], 0.5)
            qk, rest = in_proj(hn, w_in[l], ct, s1, s2, rope_cols=2 * group, scaled_cols=group,
                               q_scale=DIFF_QK_DIM ** -0.5 * LOG2_E, sb_q_cols=(3 * group, 4 * group),
                               sb_q_scale=HEAD_DIM ** -0.5 * LOG2_E)
            diff_out = diff_attention((qk, qk, rest), lambda_q1[l], lambda_k1[l], lambda_q2[l], lambda_k2[l],
                                      diff_subln[l], n_heads=n_heads, q_col=0, k_col=group, v_col=0,
                                      lambda_init=lambda_init)
            sb_out = sb_attention((rest, rest, rest), sb_norm[l], n_heads=n_heads, q_col=group, k_col=2 * group,
                                  v_col=3 * group)
            y = out_proj(diff_out, sb_out, cast(w_out))
            kv = mem_kv(mem[bi], mem_norm[l], cast(xattn_w_kv))
            h, hn = xattn_block(h, y, mix_norm_post[l], xattn_norm_pre[l], cast(xattn_w_q), kv, cast(xattn_w_o),
                                xattn_norm_post[l], ffn2_norm_pre[l])
            act = ffn_up(hn, ffn2_w_gate[l], ffn2_w_up[l])
            y = matmul_kblocked(act, ffn2_w_down[l])
            if l + 1 < depth:
                h, hn = post_pre(h, y, ffn2_norm_post[l], ffn1_norm_pre[l + 1], 0.5)
            else:
                h = post_pre(h, y, ffn2_norm_post[l], None, 0.5)
        outs.append(h)
    return jnp.stack(outs, axis=0)
```

```python
import functools
import math

import jax
import jax.numpy as jnp
from jax import lax
from jax.experimental import pallas as pl
from jax.experimental.pallas import tpu as pltpu

F32 = jnp.float32
BF16 = jnp.bfloat16

HEAD_DIM = 128
DIFF_QK_DIM = 64
ROT_DIM = 16
ROPE_THETA = 500000.0
NORM_EPS = 1e-6
XATTN_HEADS = 4
LANES = 128

VMEM_LIMIT_BYTES = 56 * 1024 * 1024
SB_EXIT = 100.0
LOG2_E = math.log2(math.e)


def _params(*sem):
    return pltpu.CompilerParams(dimension_semantics=sem, vmem_limit_bytes=VMEM_LIMIT_BYTES)


def _rms(x):
    return x * lax.rsqrt(jnp.mean(x * x, axis=-1, keepdims=True) + NORM_EPS)


def _prenorm_kernel(x_ref, g_ref, o_ref):
    o_ref[...] = (_rms(x_ref[...]) * g_ref[...]).astype(o_ref.dtype)


def prenorm(x, gain, *, tm=256):
    m, d = x.shape
    return pl.pallas_call(
        _prenorm_kernel,
        grid=(m // tm,),
        in_specs=[pl.BlockSpec((tm, d), lambda i: (i, 0)), pl.BlockSpec((1, d), lambda i: (0, 0))],
        out_specs=pl.BlockSpec((tm, d), lambda i: (i, 0)),
        out_shape=jax.ShapeDtypeStruct((m, d), BF16),
        compiler_params=_params("parallel"),
        name="prenorm",
    )(x, gain.reshape(1, d))


def _post_pre_kernel(h_ref, y_ref, gpost_ref, gpre_ref, hout_ref, hn_ref, *, coef):
    h = h_ref[...] + coef * (_rms(y_ref[...]) * gpost_ref[...])
    hout_ref[...] = h
    hn_ref[...] = (_rms(h) * gpre_ref[...]).astype(hn_ref.dtype)


def _post_kernel(h_ref, y_ref, gpost_ref, hout_ref, *, coef):
    hout_ref[...] = h_ref[...] + coef * (_rms(y_ref[...]) * gpost_ref[...])


def post_pre(h, y, gpost, gpre, coef, *, tm=256):
    m, d = h.shape
    row = pl.BlockSpec((tm, d), lambda i: (i, 0))
    vec = pl.BlockSpec((1, d), lambda i: (0, 0))
    if gpre is None:
        return pl.pallas_call(
            functools.partial(_post_kernel, coef=coef),
            grid=(m // tm,),
            in_specs=[row, row, vec],
            out_specs=row,
            out_shape=jax.ShapeDtypeStruct((m, d), F32),
            compiler_params=_params("parallel"),
            name="post_norm",
        )(h, y, gpost.reshape(1, d))
    return pl.pallas_call(
        functools.partial(_post_pre_kernel, coef=coef),
        grid=(m // tm,),
        in_specs=[row, row, vec, vec],
        out_specs=[row, row],
        out_shape=[jax.ShapeDtypeStruct((m, d), F32), jax.ShapeDtypeStruct((m, d), BF16)],
        compiler_params=_params("parallel"),
        name="post_pre_norm",
    )(h, y, gpost.reshape(1, d), gpre.reshape(1, d))


def _stream_weight_tiles(w_hbm, w_even, w_odd, stage, sems, compute, *, tn, n_tiles, n_rows, first_col_tile=0):
    j, i = pl.program_id(0), pl.program_id(1)
    mats = range(len(w_hbm))
    rows = w_even.shape[1] // n_rows
    has_next = j + 1 < n_tiles

    def piece_copy(tile, piece, mat, slot):
        src = w_hbm[mat].at[pl.ds(pl.multiple_of(piece * rows, rows), rows),
                            pl.ds(pl.multiple_of((first_col_tile + tile) * tn, tn), tn)]
        return pltpu.make_async_copy(src, stage.at[slot, mat], sems.at[slot, mat])

    def cast_piece(dst, piece, slot):
        for mat in mats:
            dst[mat, pl.ds(pl.multiple_of(piece * rows, rows), rows), :] = stage[slot, mat].astype(BF16)

    @pl.when((j == 0) & (i == 0))
    def _():
        for piece in range(n_rows):
            for mat in mats:
                piece_copy(0, piece, mat, 1).start()
            for mat in mats:
                piece_copy(0, piece, mat, 1).wait()
            cast_piece(w_even, piece, 1)

    @pl.when(has_next & (i > 0))
    def _():
        for mat in mats:
            piece_copy(j + 1, i - 1, mat, (i + 1) % 2).wait()

    @pl.when(has_next)
    def _():
        for mat in mats:
            piece_copy(j + 1, i, mat, i % 2).start()

    def step(cur, nxt):
        cast_piece(nxt, (i + n_rows - 1) % n_rows, (i + 1) % 2)
        compute(cur)

    pl.when(j % 2 == 0)(functools.partial(step, w_even, w_odd))
    pl.when(j % 2 == 1)(functools.partial(step, w_odd, w_even))

    last_slot = (n_rows - 1) % 2

    @pl.when(has_next & (i == n_rows - 1))
    def _():
        for mat in mats:
            piece_copy(j + 1, n_rows - 1, mat, last_slot).wait()
        pl.when(j % 2 == 0)(functools.partial(cast_piece, w_odd, n_rows - 1, last_slot))
        pl.when(j % 2 == 1)(functools.partial(cast_piece, w_even, n_rows - 1, last_slot))


def _stream_scratch(n_mats, k, tn, n_rows):
    rows = k // n_rows
    assert k % n_rows == 0 and rows % 16 == 0
    return [
        pltpu.VMEM((n_mats, k, tn), BF16),
        pltpu.VMEM((n_mats, k, tn), BF16),
        pltpu.VMEM((2, n_mats, rows, tn), F32),
        pltpu.SemaphoreType.DMA((2, n_mats)),
    ]


def _ffn_up_kernel(x_ref, wg_hbm, wu_hbm, o_ref, w_even, w_odd, stage, sems, **tiling):
    def compute(cur):
        x = x_ref[...]
        g = jnp.dot(x, cur[0], preferred_element_type=F32)
        u = jnp.dot(x, cur[1], preferred_element_type=F32)
        o_ref[...] = (g * jax.nn.sigmoid(g) * u).astype(o_ref.dtype)

    _stream_weight_tiles((wg_hbm, wu_hbm), w_even, w_odd, stage, sems, compute, **tiling)


def ffn_up(xn, wg, wu, *, tm=1024, tn=512):
    m, k = xn.shape
    n = wg.shape[1]
    tm, tn = min(tm, m), min(tn, n)
    n_rows = m // tm
    return pl.pallas_call(
        functools.partial(_ffn_up_kernel, tn=tn, n_tiles=n // tn, n_rows=n_rows),
        grid=(n // tn, n_rows),
        in_specs=[
            pl.BlockSpec((tm, k), lambda j, i: (i, 0)),
            pl.BlockSpec(memory_space=pl.ANY),
            pl.BlockSpec(memory_space=pl.ANY),
        ],
        out_specs=pl.BlockSpec((tm, tn), lambda j, i: (i, j)),
        out_shape=jax.ShapeDtypeStruct((m, n), BF16),
        scratch_shapes=_stream_scratch(2, k, tn, n_rows),
        compiler_params=_params("arbitrary", "arbitrary"),
        name="ffn_up",
    )(xn, wg, wu)


def _mm_acc_kernel(a_ref, b_ref, o_ref):
    @pl.when(pl.program_id(2) == 0)
    def _():
        o_ref[...] = jnp.zeros_like(o_ref)

    o_ref[...] += jnp.dot(a_ref[...], b_ref[...].astype(BF16), preferred_element_type=F32)


def matmul_kblocked(a, b, *, tm=2048, tn=1024, tk=1792):
    m, k = a.shape
    n = b.shape[1]
    tm, tn, tk = min(tm, m), min(tn, n), min(tk, k)
    while k % tk:
        tk -= LANES
    assert m % tm == 0 and n % tn == 0 and tk > 0
    return pl.pallas_call(
        _mm_acc_kernel,
        grid=(m // tm, n // tn, k // tk),
        in_specs=[
            pl.BlockSpec((tm, tk), lambda i, j, l: (i, l)),
            pl.BlockSpec((tk, tn), lambda i, j, l: (l, j)),
        ],
        out_specs=pl.BlockSpec((tm, tn), lambda i, j, l: (i, j)),
        out_shape=jax.ShapeDtypeStruct((m, n), F32),
        compiler_params=_params("parallel", "parallel", "arbitrary"),
        name="matmul_kblocked",
    )(a, b)


def _out_proj_kernel(a_ref, b_ref, wa_ref, wb_ref, o_ref):
    o_ref[...] = jnp.dot(a_ref[...], wa_ref[...], preferred_element_type=F32) + jnp.dot(
        b_ref[...], wb_ref[...], preferred_element_type=F32
    )


def out_proj(a, b, w, *, tm=1024, tn=1024):
    m, ka = a.shape
    kb = b.shape[1]
    n = w.shape[1]
    tm, tn = min(tm, m), min(tn, n)
    assert ka == kb
    return pl.pallas_call(
        _out_proj_kernel,
        grid=(m // tm, n // tn),
        in_specs=[
            pl.BlockSpec((tm, ka), lambda i, j: (i, 0)),
            pl.BlockSpec((tm, kb), lambda i, j: (i, 0)),
            pl.BlockSpec((ka, tn), lambda i, j: (0, j)),
            pl.BlockSpec((kb, tn), lambda i, j: (1, j)),
        ],
        out_specs=pl.BlockSpec((tm, tn), lambda i, j: (i, j)),
        out_shape=jax.ShapeDtypeStruct((m, n), F32),
        compiler_params=_params("parallel", "arbitrary"),
        name="out_proj",
    )(a, b, w, w)


def _rope_table_kernel(pos_ref, invf_ref, ct_ref, s1_ref, s2_ref):
    ang = pos_ref[...] * invf_ref[...]
    cos, sin = jnp.cos(ang), jnp.sin(ang)
    lane = lax.broadcasted_iota(jnp.int32, ang.shape, 1) % DIFF_QK_DIM
    half = ROT_DIM // 2
    ct_ref[...] = jnp.where(lane < ROT_DIM, cos, 1.0)
    s1_ref[...] = jnp.where(lane < half, -sin, 0.0)
    s2_ref[...] = jnp.where((lane >= half) & (lane < ROT_DIM), sin, 0.0)


def rope_tables(positions_col, *, tm=1024):
    s = positions_col.shape[0]
    tm = min(tm, s)
    half = ROT_DIM // 2
    inv_freq = ROPE_THETA ** (-jnp.arange(0, ROT_DIM, 2, dtype=F32) / ROT_DIM)
    invf = jnp.tile(inv_freq, LANES // half).reshape(1, LANES)
    tab = pl.BlockSpec((tm, LANES), lambda i: (i, 0))
    return pl.pallas_call(
        _rope_table_kernel,
        grid=(s // tm,),
        in_specs=[pl.BlockSpec((tm, 1), lambda i: (i, 0)), pl.BlockSpec((1, LANES), lambda i: (0, 0))],
        out_specs=[tab, tab, tab],
        out_shape=[jax.ShapeDtypeStruct((s, LANES), F32)] * 3,
        compiler_params=_params("parallel"),
        name="rope_tables",
    )(positions_col, invf)


def _in_proj_plain_kernel(x_ref, w_hbm, o_ref, w_even, w_odd, stage, sems, *, scaled_tiles, scale, **tiling):
    def compute(cur):
        j = pl.program_id(0)
        s = jnp.where((j >= scaled_tiles[0]) & (j < scaled_tiles[1]), scale, 1.0).astype(F32)
        o_ref[...] = (jnp.dot(x_ref[...], cur[0], preferred_element_type=F32) * s).astype(o_ref.dtype)

    _stream_weight_tiles((w_hbm,), w_even, w_odd, stage, sems, compute, **tiling)


def _in_proj_rope_kernel(x_ref, w_hbm, ct_ref, s1_ref, s2_ref, o_ref, w_even, w_odd, stage, sems, *,
                         n_scaled_tiles, q_scale, **tiling):
    def compute(cur):
        scale = jnp.where(pl.program_id(0) < n_scaled_tiles, q_scale, 1.0).astype(F32)
        half = ROT_DIM // 2
        hm = x_ref.shape[0] // 2
        for rows in (slice(0, hm), slice(hm, 2 * hm)):
            acc = jnp.dot(x_ref[rows, :], cur[0], preferred_element_type=F32)
            ct, s1, s2 = ct_ref[rows, :], s1_ref[rows, :], s2_ref[rows, :]
            for c in range(acc.shape[1] // LANES):
                a = acc[:, c * LANES:(c + 1) * LANES]
                r = a * ct + pltpu.roll(a, LANES - half, 1) * s1 + pltpu.roll(a, half, 1) * s2
                o_ref[rows, c * LANES:(c + 1) * LANES] = (r * scale).astype(o_ref.dtype)

    _stream_weight_tiles((w_hbm,), w_even, w_odd, stage, sems, compute, **tiling)


def in_proj(xn, w, ct, s1, s2, *, rope_cols, scaled_cols, q_scale, sb_q_cols, sb_q_scale, tm=1024, tn=1024):
    m, k = xn.shape
    n = w.shape[1]
    tm, tn = min(tm, m), min(tn, n, scaled_cols)
    assert rope_cols % tn == 0 and scaled_cols % tn == 0
    assert all((c - rope_cols) % tn == 0 for c in sb_q_cols)
    sb_tiles = tuple((c - rope_cols) // tn for c in sb_q_cols)
    n_rope_tiles, n_plain_tiles, n_rows = rope_cols // tn, (n - rope_cols) // tn, m // tm
    x_spec = pl.BlockSpec((tm, k), lambda j, i: (i, 0))
    w_spec = pl.BlockSpec(memory_space=pl.ANY)
    out_spec = pl.BlockSpec((tm, tn), lambda j, i: (i, j))
    tab = pl.BlockSpec((tm, LANES), lambda j, i: (i, 0))
    roped = pl.pallas_call(
        functools.partial(_in_proj_rope_kernel, n_scaled_tiles=scaled_cols // tn, q_scale=q_scale,
                          tn=tn, n_tiles=n_rope_tiles, n_rows=n_rows),
        grid=(n_rope_tiles, n_rows),
        in_specs=[x_spec, w_spec, tab, tab, tab],
        out_specs=out_spec,
        out_shape=jax.ShapeDtypeStruct((m, rope_cols), BF16),
        scratch_shapes=_stream_scratch(1, k, tn, n_rows),
        compiler_params=_params("arbitrary", "arbitrary"),
        name="in_proj_rope",
    )(xn, w, ct, s1, s2)
    plain = pl.pallas_call(
        functools.partial(_in_proj_plain_kernel, scaled_tiles=sb_tiles, scale=sb_q_scale,
                          tn=tn, n_tiles=n_plain_tiles, n_rows=n_rows, first_col_tile=n_rope_tiles),
        grid=(n_plain_tiles, n_rows),
        in_specs=[x_spec, w_spec],
        out_specs=out_spec,
        out_shape=jax.ShapeDtypeStruct((m, n - rope_cols), BF16),
        scratch_shapes=_stream_scratch(1, k, tn, n_rows),
        compiler_params=_params("arbitrary", "arbitrary"),
        name="in_proj_plain",
    )(xn, w)
    return roped, plain


def _qk(q, k):
    return lax.dot_general(q, k, (((1,), (1,)), ((), ())), preferred_element_type=F32)


def _diff_attn_kernel(q_ref, k_ref, v_ref, lq1_ref, lk1_ref, lq2_ref, lk2_ref, g_ref, o_ref,
                      vp_ref, s_ref, m_ref, acc_ref, *, tq, lambda_init):
    tk = tq
    pair = pl.program_id(1)

    @pl.when(pair == 0)
    def _():
        vp_ref[:, :HEAD_DIM] = v_ref[...]
        vp_ref[:, HEAD_DIM:] = jnp.ones((v_ref.shape[0], HEAD_DIM), BF16)

    lane_q = lax.broadcasted_iota(jnp.int32, (tq, HEAD_DIM), 1)
    lane = lax.broadcasted_iota(jnp.int32, (2 * tq, LANES), 1)
    row = lax.broadcasted_iota(jnp.int32, (2 * tq, LANES), 0)
    row_in_tile = jnp.where(row >= tq, row - tq, row)

    class Tile:
        def __init__(self, t):
            self.t = t
            self.n_blocks = 2 * pair + t + 1
            q = q_ref[t * tq:(t + 1) * tq, :]
            zero = jnp.zeros_like(q)
            self.q_st = jnp.concatenate(
                [jnp.where(lane_q < DIFF_QK_DIM, q, zero), jnp.where(lane_q >= DIFF_QK_DIM, q, zero)], axis=0)
            m_ref[t] = jnp.full((2 * tq, LANES), -jnp.inf, F32)
            acc_ref[t] = jnp.zeros((2 * tq, 2 * HEAD_DIM), F32)

        def key_start(self, b):
            return pl.multiple_of(jnp.maximum(self.n_blocks - 1 - b, 0) * tk, tk)

        def scores(self, b, slot):
            s_ref[self.t, slot] = _qk(self.q_st, k_ref[pl.ds(self.key_start(b), tk), :])

        def softmax_sum(self, b, slot, diagonal):
            t = self.t

            def chunk(c):
                s_c = s_ref[t, slot, :, c * LANES:(c + 1) * LANES]
                return jnp.where(lane + c * LANES <= row_in_tile, s_c, -jnp.inf) if diagonal else s_c

            m_blk = chunk(0)
            for c in range(1, tk // LANES):
                m_blk = jnp.maximum(m_blk, chunk(c))
            m_old = m_ref[t]
            m_new = jnp.maximum(m_old, jnp.max(m_blk, axis=-1, keepdims=True))
            alpha = jnp.exp2(m_old - m_new)
            m_ref[t] = m_new
            p = jnp.concatenate([jnp.exp2(chunk(c) - m_new).astype(BF16) for c in range(tk // LANES)], axis=1)
            pv = jnp.dot(p, vp_ref[pl.ds(self.key_start(b), tk), :], preferred_element_type=F32)
            acc_ref[t, :, :HEAD_DIM] = alpha * acc_ref[t, :, :HEAD_DIM] + pv[:, :HEAD_DIM]
            acc_ref[t, :, HEAD_DIM:] = alpha * acc_ref[t, :, HEAD_DIM:] + pv[:, HEAD_DIM:]

        def finish(self, lam):
            acc = acc_ref[self.t]
            o1 = acc[:tq, :HEAD_DIM] / acc[:tq, HEAD_DIM:]
            o2 = acc[tq:, :HEAD_DIM] / acc[tq:, HEAD_DIM:]
            out = o1 - lam * o2
            o_ref[self.t * tq:(self.t + 1) * tq, :] = (
                (_rms(out) * g_ref[...]) * (1.0 - lambda_init)).astype(o_ref.dtype)

    tiles = (Tile(0), Tile(1))
    for tl in tiles:
        tl.scores(0, 0)
    for tl in tiles:
        tl.scores(1, 1)
    for tl in tiles:
        tl.softmax_sum(0, 0, True)

    def body(j, carry):
        b = 2 * j + 1
        for tl in tiles:
            tl.scores(b + 1, 0)
            tl.softmax_sum(b, 1, False)
        for tl in tiles:
            tl.scores(b + 2, 1)
            tl.softmax_sum(b + 1, 0, False)
        return carry

    lax.fori_loop(0, pair, body, 0)
    tiles[1].softmax_sum(2 * pair + 1, 1, False)

    lam = (jnp.exp(jnp.sum(lq1_ref[...] * lk1_ref[...], keepdims=True))
           - jnp.exp(jnp.sum(lq2_ref[...] * lk2_ref[...], keepdims=True)) + lambda_init)
    for tl in tiles:
        tl.finish(lam)


def diff_attention(qkv, lq1, lk1, lq2, lk2, subln, *, n_heads, q_col, k_col, v_col, lambda_init, tq=512):
    s = qkv[0].shape[0]
    tq = min(tq, s // 2)
    assert s % (2 * tq) == 0
    qb, kb, vb = q_col // HEAD_DIM, k_col // HEAD_DIM, v_col // HEAD_DIM
    lam_spec = pl.BlockSpec((1, DIFF_QK_DIM), lambda h, i: (0, 0))
    return pl.pallas_call(
        functools.partial(_diff_attn_kernel, tq=tq, lambda_init=lambda_init),
        grid=(n_heads, s // (2 * tq)),
        in_specs=[
            pl.BlockSpec((2 * tq, HEAD_DIM), lambda h, i: (i, qb + h)),
            pl.BlockSpec((s, HEAD_DIM), lambda h, i: (0, kb + h)),
            pl.BlockSpec((s, HEAD_DIM), lambda h, i: (0, vb + h)),
            lam_spec, lam_spec, lam_spec, lam_spec,
            pl.BlockSpec((1, HEAD_DIM), lambda h, i: (0, 0)),
        ],
        out_specs=pl.BlockSpec((2 * tq, HEAD_DIM), lambda h, i: (i, h)),
        out_shape=jax.ShapeDtypeStruct((s, n_heads * HEAD_DIM), BF16),
        scratch_shapes=[
            pltpu.VMEM((s, 2 * HEAD_DIM), BF16),
            pltpu.VMEM((2, 2, 2 * tq, tq), F32),
            pltpu.VMEM((2, 2 * tq, LANES), F32),
            pltpu.VMEM((2, 2 * tq, 2 * HEAD_DIM), F32),
        ],
        compiler_params=_params("arbitrary", "arbitrary"),
        name="diff_attention",
    )(*qkv, lq1.reshape(1, -1), lk1.reshape(1, -1), lq2.reshape(1, -1), lk2.reshape(1, -1), subln.reshape(1, -1))


def _sb_attn_kernel(q_ref, k_ref, v_ref, tri_ref, g_ref, o_ref, c_ref, acc_ref, *, tq, tk, heads):
    i = pl.program_id(1)
    tri = tri_ref[...]

    def block(h, kb, masked, c):
        cols = slice(h * HEAD_DIM, (h + 1) * HEAD_DIM)
        start = pl.multiple_of(kb * tk, tk)
        k = k_ref[pl.ds(start, tk), cols]
        v = v_ref[pl.ds(start, tk), cols]
        z = _qk(q_ref[:, cols], k)
        nz = -z
        lsm = jnp.minimum(nz, 0.0) - jnp.log2(1.0 + jnp.exp2(jnp.minimum(z, nz)))
        if masked:
            q_pos = i * tq + lax.broadcasted_iota(jnp.int32, z.shape, 0)
            k_pos = kb * tk + lax.broadcasted_iota(jnp.int32, z.shape, 1)
            strict = k_pos < q_pos
            lsm = jnp.where(strict, lsm, 0.0)
        hi = lsm.astype(BF16)
        lo = (lsm - hi.astype(F32)).astype(BF16)
        r = jnp.dot(hi, tri, preferred_element_type=F32) + jnp.dot(lo, tri, preferred_element_type=F32)
        a = jnp.exp2(z + r + c)
        if masked:
            a = jnp.where(strict, a, 0.0)
        return jnp.dot(a.astype(BF16), v, preferred_element_type=F32), c + r[:, 0:1]

    n_diag = tq // tk
    kb_top = (i + 1) * n_diag - 1

    def head_blocks(with_left):
        for h in range(heads):
            c = jnp.zeros((tq, 1), F32)
            acc = None
            for d in range(n_diag + int(with_left)):
                pv, c = block(h, kb_top - d, d < n_diag, c)
                acc = pv if acc is None else acc + pv
            acc_ref[h] = acc
            c_ref[h] = c

    pl.when(i == 0)(functools.partial(head_blocks, False))
    pl.when(i > 0)(functools.partial(head_blocks, True))

    def cond(carry):
        kb, cm = carry
        return jnp.logical_and(kb >= 0, cm > -SB_EXIT * LOG2_E)

    def body(carry):
        kb, _ = carry
        for h in range(heads):
            pv, c = block(h, kb, False, c_ref[h])
            acc_ref[h] += pv
            c_ref[h] = c
        return kb - 1, jnp.max(c_ref[...])

    lax.while_loop(cond, body, (kb_top - n_diag - 1, jnp.max(c_ref[...])))
    for h in range(heads):
        o_ref[:, h * HEAD_DIM:(h + 1) * HEAD_DIM] = (_rms(acc_ref[h]) * g_ref[...]).astype(o_ref.dtype)


def sb_attention(qkv, gain, *, n_heads, q_col, k_col, v_col, tq=256, tk=256, heads=4):
    s = qkv[0].shape[0]
    tq, tk, heads = min(tq, s), min(tk, s), min(heads, n_heads)
    width = heads * HEAD_DIM
    assert tq % tk == 0 and n_heads % heads == 0 and all(c % width == 0 for c in (q_col, k_col, v_col))
    qb, kb, vb = q_col // width, k_col // width, v_col // width
    tri = (lax.broadcasted_iota(jnp.int32, (tk, tk), 0) >= lax.broadcasted_iota(jnp.int32, (tk, tk), 1)).astype(BF16)
    return pl.pallas_call(
        functools.partial(_sb_attn_kernel, tq=tq, tk=tk, heads=heads),
        grid=(n_heads // heads, s // tq),
        in_specs=[
            pl.BlockSpec((tq, width), lambda g, i: (i, qb + g)),
            pl.BlockSpec((s, width), lambda g, i: (0, kb + g)),
            pl.BlockSpec((s, width), lambda g, i: (0, vb + g)),
            pl.BlockSpec((tk, tk), lambda g, i: (0, 0)),
            pl.BlockSpec((1, HEAD_DIM), lambda g, i: (0, 0)),
        ],
        out_specs=pl.BlockSpec((tq, width), lambda g, i: (i, g)),
        out_shape=jax.ShapeDtypeStruct((s, n_heads * HEAD_DIM), BF16),
        scratch_shapes=[pltpu.VMEM((heads, tq, 1), F32), pltpu.VMEM((heads, tq, HEAD_DIM), F32)],
        compiler_params=_params("parallel", "arbitrary"),
        name="sb_attention",
    )(*qkv, tri, gain.reshape(1, -1))


def _mem_kv_kernel(mem_ref, g_ref, w_ref, o_ref):
    mn = (_rms(mem_ref[...]) * g_ref[...]).astype(BF16)
    o_ref[...] = jnp.dot(mn, w_ref[...], preferred_element_type=F32).astype(o_ref.dtype)


def mem_kv(mem, gain, w_kv):
    mt, d = mem.shape
    n = w_kv.shape[1]
    return pl.pallas_call(
        _mem_kv_kernel,
        grid=(1,),
        in_specs=[pl.BlockSpec((mt, d), lambda i: (0, 0)), pl.BlockSpec((1, d), lambda i: (0, 0)),
                  pl.BlockSpec((d, n), lambda i: (0, 0))],
        out_specs=pl.BlockSpec((mt, n), lambda i: (0, 0)),
        out_shape=jax.ShapeDtypeStruct((mt, n), BF16),
        compiler_params=_params("arbitrary"),
        name="mem_kv",
    )(mem, gain.reshape(1, d), w_kv)


def _xattn_kernel(h_ref, y_ref, gmix_ref, gpre_ref, wq_ref, kv_ref, wo_ref, gpost_ref, gnext_ref, hout_ref, hn_ref, *,
                  scale):
    h = h_ref[...] + _rms(y_ref[...]) * gmix_ref[...]
    hn = (_rms(h) * gpre_ref[...]).astype(BF16)
    q = jnp.dot(hn, wq_ref[...], preferred_element_type=F32).astype(BF16)
    width = XATTN_HEADS * HEAD_DIM
    outs = []
    for hd in range(XATTN_HEADS):
        cols = slice(hd * HEAD_DIM, (hd + 1) * HEAD_DIM)
        k = kv_ref[:, cols]
        v = kv_ref[:, width + hd * HEAD_DIM:width + (hd + 1) * HEAD_DIM]
        sc = _qk(q[:, cols], k) * scale
        sc = sc - jnp.max(sc, axis=-1, keepdims=True)
        e = jnp.exp(sc)
        p = e / jnp.sum(e, axis=-1, keepdims=True)
        outs.append(jnp.dot(p.astype(BF16), v, preferred_element_type=F32).astype(BF16))
    o = jnp.concatenate(outs, axis=-1)
    y = jnp.dot(o, wo_ref[...], preferred_element_type=F32)
    h2 = h + _rms(y) * gpost_ref[...]
    hout_ref[...] = h2
    hn_ref[...] = (_rms(h2) * gnext_ref[...]).astype(hn_ref.dtype)


def xattn_block(h, y_mix, gmix, gpre, wq, kv, wo, gpost, gnext, *, tm=256):
    m, d = h.shape
    tm = min(tm, m)
    width = wq.shape[1]
    row = pl.BlockSpec((tm, d), lambda i: (i, 0))
    vec = pl.BlockSpec((1, d), lambda i: (0, 0))
    return pl.pallas_call(
        functools.partial(_xattn_kernel, scale=HEAD_DIM ** -0.5),
        grid=(m // tm,),
        in_specs=[row, row, vec, vec, pl.BlockSpec((d, width), lambda i: (0, 0)),
                  pl.BlockSpec(kv.shape, lambda i: (0, 0)), pl.BlockSpec((width, d), lambda i: (0, 0)), vec, vec],
        out_specs=[row, row],
        out_shape=[jax.ShapeDtypeStruct((m, d), F32), jax.ShapeDtypeStruct((m, d), BF16)],
        compiler_params=_params("parallel"),
        name="xattn_block",
    )(h, y_mix, gmix.reshape(1, d), gpre.reshape(1, d), wq, kv, wo, gpost.reshape(1, d), gnext.reshape(1, d))


def kernel(x, mem, positions, ffn1_norm_pre, ffn1_norm_post, ffn1_w_gate, ffn1_w_up, ffn1_w_down, mix_norm_pre, mix_norm_post, w_in, w_out, lambda_q1, lambda_k1, lambda_q2, lambda_k2, diff_subln, sb_norm, xattn_norm_pre, xattn_norm_post, mem_norm, xattn_w_q, xattn_w_kv, xattn_w_o, ffn2_norm_pre, ffn2_norm_post, ffn2_w_gate, ffn2_w_up, ffn2_w_down):
    b, s, d = x.shape
    depth = w_in.shape[0]
    in_width = w_in.shape[2]
    group = in_width // 6
    n_heads = group // HEAD_DIM
    outs = []
    for bi in range(b):
        h = x[bi]
        pos_col = positions[bi].astype(F32).reshape(s, 1)
        ct, s1, s2 = rope_tables(pos_col)
        hn = prenorm(h, ffn1_norm_pre[0])
        for l in range(depth):
            lambda_init = 0.8 - 0.6 * math.exp(-0.3 * l)
            cast = lambda w: w[l].astype(BF16)
            act = ffn_up(hn, ffn1_w_gate[l], ffn1_w_up[l])
            y = matmul_kblocked(act, ffn1_w_down[l])
            h, hn = post_pre(h, y, ffn1_norm_post[l], mix_norm_pre[l], 0.5)
            qk, rest = in_proj(hn, w_in[l], ct, s1, s2, rope_cols=2 * group, scaled_cols=group,
                               q_scale=DIFF_QK_DIM ** -0.5 * LOG2_E, sb_q_cols=(3 * group, 4 * group),
                               sb_q_scale=HEAD_DIM ** -0.5 * LOG2_E)
            diff_out = diff_attention((qk, qk, rest), lambda_q1[l], lambda_k1[l], lambda_q2[l], lambda_k2[l],
                                      diff_subln[l], n_heads=n_heads, q_col=0, k_col=group, v_col=0,
                                      lambda_init=lambda_init)
            sb_out = sb_attention((rest, rest, rest), sb_norm[l], n_heads=n_heads, q_col=group, k_col=2 * group,
                                  v_col=3 * group)
            y = out_proj(diff_out, sb_out, cast(w_out))
            kv = mem_kv(mem[bi], mem_norm[l], cast(xattn_w_kv))
            h, hn = xattn_block(h, y, mix_norm_post[l], xattn_norm_pre[l], cast(xattn_w_q), kv, cast(xattn_w_o),
                                xattn_norm_post[l], ffn2_norm_pre[l])
            act = ffn_up(hn, ffn2_w_gate[l], ffn2_w_up[l])
            y = matmul_kblocked(act, ffn2_w_down[l])
            if l + 1 < depth:
                h, hn = post_pre(h, y, ffn2_norm_post[l], ffn1_norm_pre[l + 1], 0.5)
            else:
                h = post_pre(h, y, ffn2_norm_post[l], None, 0.5)
        outs.append(h)
    return jnp.stack(outs, axis=0)
```

```python
import functools
import math

import jax
import jax.numpy as jnp
from jax import lax
from jax.experimental import pallas as pl
from jax.experimental.pallas import tpu as pltpu

F32 = jnp.float32
BF16 = jnp.bfloat16

HEAD_DIM = 128
DIFF_QK_DIM = 64
ROT_DIM = 16
ROPE_THETA = 500000.0
NORM_EPS = 1e-6
XATTN_HEADS = 4
LANES = 128

VMEM_LIMIT_BYTES = 56 * 1024 * 1024
SB_EXIT = 100.0
LOG2_E = math.log2(math.e)


def _params(*sem):
    return pltpu.CompilerParams(dimension_semantics=sem, vmem_limit_bytes=VMEM_LIMIT_BYTES)


def _rms(x):
    return x * lax.rsqrt(jnp.mean(x * x, axis=-1, keepdims=True) + NORM_EPS)


def _prenorm_kernel(x_ref, g_ref, o_ref):
    o_ref[...] = (_rms(x_ref[...]) * g_ref[...]).astype(o_ref.dtype)


def prenorm(x, gain, *, tm=256):
    m, d = x.shape
    return pl.pallas_call(
        _prenorm_kernel,
        grid=(m // tm,),
        in_specs=[pl.BlockSpec((tm, d), lambda i: (i, 0)), pl.BlockSpec((1, d), lambda i: (0, 0))],
        out_specs=pl.BlockSpec((tm, d), lambda i: (i, 0)),
        out_shape=jax.ShapeDtypeStruct((m, d), BF16),
        compiler_params=_params("parallel"),
        name="prenorm",
    )(x, gain.reshape(1, d))


def _post_pre_kernel(h_ref, y_ref, gpost_ref, gpre_ref, hout_ref, hn_ref, *, coef):
    h = h_ref[...] + coef * (_rms(y_ref[...]) * gpost_ref[...])
    hout_ref[...] = h
    hn_ref[...] = (_rms(h) * gpre_ref[...]).astype(hn_ref.dtype)


def _post_kernel(h_ref, y_ref, gpost_ref, hout_ref, *, coef):
    hout_ref[...] = h_ref[...] + coef * (_rms(y_ref[...]) * gpost_ref[...])


def post_pre(h, y, gpost, gpre, coef, *, tm=256):
    m, d = h.shape
    row = pl.BlockSpec((tm, d), lambda i: (i, 0))
    vec = pl.BlockSpec((1, d), lambda i: (0, 0))
    if gpre is None:
        return pl.pallas_call(
            functools.partial(_post_kernel, coef=coef),
            grid=(m // tm,),
            in_specs=[row, row, vec],
            out_specs=row,
            out_shape=jax.ShapeDtypeStruct((m, d), F32),
            compiler_params=_params("parallel"),
            name="post_norm",
        )(h, y, gpost.reshape(1, d))
    return pl.pallas_call(
        functools.partial(_post_pre_kernel, coef=coef),
        grid=(m // tm,),
        in_specs=[row, row, vec, vec],
        out_specs=[row, row],
        out_shape=[jax.ShapeDtypeStruct((m, d), F32), jax.ShapeDtypeStruct((m, d), BF16)],
        compiler_params=_params("parallel"),
        name="post_pre_norm",
    )(h, y, gpost.reshape(1, d), gpre.reshape(1, d))


def _stream_weight_tiles(w_hbm, w_even, w_odd, stage, sems, compute, *, tn, n_tiles, n_rows, first_col_tile=0):
    j, i = pl.program_id(0), pl.program_id(1)
    mats = range(len(w_hbm))
    rows = w_even.shape[1] // n_rows
    has_next = j + 1 < n_tiles

    def piece_copy(tile, piece, mat, slot):
        src = w_hbm[mat].at[pl.ds(pl.multiple_of(piece * rows, rows), rows),
                            pl.ds(pl.multiple_of((first_col_tile + tile) * tn, tn), tn)]
        return pltpu.make_async_copy(src, stage.at[slot, mat], sems.at[slot, mat])

    def cast_piece(dst, piece, slot):
        for mat in mats:
            dst[mat, pl.ds(pl.multiple_of(piece * rows, rows), rows), :] = stage[slot, mat].astype(BF16)

    @pl.when((j == 0) & (i == 0))
    def _():
        for piece in range(n_rows):
            for mat in mats:
                piece_copy(0, piece, mat, 1).start()
            for mat in mats:
                piece_copy(0, piece, mat, 1).wait()
            cast_piece(w_even, piece, 1)

    @pl.when(has_next & (i > 0))
    def _():
        for mat in mats:
            piece_copy(j + 1, i - 1, mat, (i + 1) % 2).wait()

    @pl.when(has_next)
    def _():
        for mat in mats:
            piece_copy(j + 1, i, mat, i % 2).start()

    def step(cur, nxt):
        cast_piece(nxt, (i + n_rows - 1) % n_rows, (i + 1) % 2)
        compute(cur)

    pl.when(j % 2 == 0)(functools.partial(step, w_even, w_odd))
    pl.when(j % 2 == 1)(functools.partial(step, w_odd, w_even))

    last_slot = (n_rows - 1) % 2

    @pl.when(has_next & (i == n_rows - 1))
    def _():
        for mat in mats:
            piece_copy(j + 1, n_rows - 1, mat, last_slot).wait()
        pl.when(j % 2 == 0)(functools.partial(cast_piece, w_odd, n_rows - 1, last_slot))
        pl.when(j % 2 == 1)(functools.partial(cast_piece, w_even, n_rows - 1, last_slot))


def _stream_scratch(n_mats, k, tn, n_rows):
    rows = k // n_rows
    assert k % n_rows == 0 and rows % 16 == 0
    return [
        pltpu.VMEM((n_mats, k, tn), BF16),
        pltpu.VMEM((n_mats, k, tn), BF16),
        pltpu.VMEM((2, n_mats, rows, tn), F32),
        pltpu.SemaphoreType.DMA((2, n_mats)),
    ]


def _ffn_up_kernel(x_ref, wg_hbm, wu_hbm, o_ref, w_even, w_odd, stage, sems, **tiling):
    def compute(cur):
        x = x_ref[...]
        g = jnp.dot(x, cur[0], preferred_element_type=F32)
        u = jnp.dot(x, cur[1], preferred_element_type=F32)
        o_ref[...] = (g * jax.nn.sigmoid(g) * u).astype(o_ref.dtype)

    _stream_weight_tiles((wg_hbm, wu_hbm), w_even, w_odd, stage, sems, compute, **tiling)


def ffn_up(xn, wg, wu, *, tm=1024, tn=512):
    m, k = xn.shape
    n = wg.shape[1]
    tm, tn = min(tm, m), min(tn, n)
    n_rows = m // tm
    return pl.pallas_call(
        functools.partial(_ffn_up_kernel, tn=tn, n_tiles=n // tn, n_rows=n_rows),
        grid=(n // tn, n_rows),
        in_specs=[
            pl.BlockSpec((tm, k), lambda j, i: (i, 0)),
            pl.BlockSpec(memory_space=pl.ANY),
            pl.BlockSpec(memory_space=pl.ANY),
        ],
        out_specs=pl.BlockSpec((tm, tn), lambda j, i: (i, j)),
        out_shape=jax.ShapeDtypeStruct((m, n), BF16),
        scratch_shapes=_stream_scratch(2, k, tn, n_rows),
        compiler_params=_params("arbitrary", "arbitrary"),
        name="ffn_up",
    )(xn, wg, wu)


def _mm_acc_kernel(a_ref, b_ref, o_ref):
    @pl.when(pl.program_id(2) == 0)
    def _():
        o_ref[...] = jnp.zeros_like(o_ref)

    o_ref[...] += jnp.dot(a_ref[...], b_ref[...].astype(BF16), preferred_element_type=F32)


def matmul_kblocked(a, b, *, tm=2048, tn=1024, tk=1792):
    m, k = a.shape
    n = b.shape[1]
    tm, tn, tk = min(tm, m), min(tn, n), min(tk, k)
    while k % tk:
        tk -= LANES
    assert m % tm == 0 and n % tn == 0 and tk > 0
    return pl.pallas_call(
        _mm_acc_kernel,
        grid=(m // tm, n // tn, k // tk),
        in_specs=[
            pl.BlockSpec((tm, tk), lambda i, j, l: (i, l)),
            pl.BlockSpec((tk, tn), lambda i, j, l: (l, j)),
        ],
        out_specs=pl.BlockSpec((tm, tn), lambda i, j, l: (i, j)),
        out_shape=jax.ShapeDtypeStruct((m, n), F32),
        compiler_params=_params("parallel", "parallel", "arbitrary"),
        name="matmul_kblocked",
    )(a, b)


def _out_proj_kernel(a_ref, b_ref, w_hbm, o_ref, w_even, w_odd, stage, sems, **tiling):
    ka = a_ref.shape[1]

    def compute(cur):
        o_ref[...] = (jnp.dot(a_ref[...], cur[0, :ka, :], preferred_element_type=F32)
                      + jnp.dot(b_ref[...], cur[0, ka:, :], preferred_element_type=F32))

    _stream_weight_tiles((w_hbm,), w_even, w_odd, stage, sems, compute, **tiling)


def out_proj(a, b, w, *, tm=1024, tn=1024):
    m, ka = a.shape
    kb = b.shape[1]
    n = w.shape[1]
    tm, tn = min(tm, m), min(tn, n)
    n_rows = m // tm
    assert w.shape[0] == ka + kb
    return pl.pallas_call(
        functools.partial(_out_proj_kernel, tn=tn, n_tiles=n // tn, n_rows=n_rows),
        grid=(n // tn, n_rows),
        in_specs=[
            pl.BlockSpec((tm, ka), lambda j, i: (i, 0)),
            pl.BlockSpec((tm, kb), lambda j, i: (i, 0)),
            pl.BlockSpec(memory_space=pl.ANY),
        ],
        out_specs=pl.BlockSpec((tm, tn), lambda j, i: (i, j)),
        out_shape=jax.ShapeDtypeStruct((m, n), F32),
        scratch_shapes=_stream_scratch(1, ka + kb, tn, n_rows),
        compiler_params=_params("arbitrary", "arbitrary"),
        name="out_proj",
    )(a, b, w)


def _rope_table_kernel(pos_ref, invf_ref, ct_ref, s1_ref, s2_ref):
    ang = pos_ref[...] * invf_ref[...]
    cos, sin = jnp.cos(ang), jnp.sin(ang)
    lane = lax.broadcasted_iota(jnp.int32, ang.shape, 1) % DIFF_QK_DIM
    half = ROT_DIM // 2
    ct_ref[...] = jnp.where(lane < ROT_DIM, cos, 1.0)
    s1_ref[...] = jnp.where(lane < half, -sin, 0.0)
    s2_ref[...] = jnp.where((lane >= half) & (lane < ROT_DIM), sin, 0.0)


def rope_tables(positions_col, *, tm=1024):
    s = positions_col.shape[0]
    tm = min(tm, s)
    half = ROT_DIM // 2
    inv_freq = ROPE_THETA ** (-jnp.arange(0, ROT_DIM, 2, dtype=F32) / ROT_DIM)
    invf = jnp.tile(inv_freq, LANES // half).reshape(1, LANES)
    tab = pl.BlockSpec((tm, LANES), lambda i: (i, 0))
    return pl.pallas_call(
        _rope_table_kernel,
        grid=(s // tm,),
        in_specs=[pl.BlockSpec((tm, 1), lambda i: (i, 0)), pl.BlockSpec((1, LANES), lambda i: (0, 0))],
        out_specs=[tab, tab, tab],
        out_shape=[jax.ShapeDtypeStruct((s, LANES), F32)] * 3,
        compiler_params=_params("parallel"),
        name="rope_tables",
    )(positions_col, invf)


def _in_proj_plain_kernel(x_ref, w_hbm, o_ref, w_even, w_odd, stage, sems, *, scaled_tiles, scale, **tiling):
    def compute(cur):
        j = pl.program_id(0)
        s = jnp.where((j >= scaled_tiles[0]) & (j < scaled_tiles[1]), scale, 1.0).astype(F32)
        o_ref[...] = (jnp.dot(x_ref[...], cur[0], preferred_element_type=F32) * s).astype(o_ref.dtype)

    _stream_weight_tiles((w_hbm,), w_even, w_odd, stage, sems, compute, **tiling)


def _in_proj_rope_kernel(x_ref, w_hbm, ct_ref, s1_ref, s2_ref, o_ref, w_even, w_odd, stage, sems, *,
                         n_scaled_tiles, q_scale, **tiling):
    def compute(cur):
        scale = jnp.where(pl.program_id(0) < n_scaled_tiles, q_scale, 1.0).astype(F32)
        half = ROT_DIM // 2
        hm = x_ref.shape[0] // 2
        for rows in (slice(0, hm), slice(hm, 2 * hm)):
            acc = jnp.dot(x_ref[rows, :], cur[0], preferred_element_type=F32)
            ct, s1, s2 = ct_ref[rows, :], s1_ref[rows, :], s2_ref[rows, :]
            for c in range(acc.shape[1] // LANES):
                a = acc[:, c * LANES:(c + 1) * LANES]
                r = a * ct + pltpu.roll(a, LANES - half, 1) * s1 + pltpu.roll(a, half, 1) * s2
                o_ref[rows, c * LANES:(c + 1) * LANES] = (r * scale).astype(o_ref.dtype)

    _stream_weight_tiles((w_hbm,), w_even, w_odd, stage, sems, compute, **tiling)


def in_proj(xn, w, ct, s1, s2, *, rope_cols, scaled_cols, q_scale, sb_q_cols, sb_q_scale, tm=1024, tn=1024):
    m, k = xn.shape
    n = w.shape[1]
    tm, tn = min(tm, m), min(tn, n, scaled_cols)
    assert rope_cols % tn == 0 and scaled_cols % tn == 0
    assert all((c - rope_cols) % tn == 0 for c in sb_q_cols)
    sb_tiles = tuple((c - rope_cols) // tn for c in sb_q_cols)
    n_rope_tiles, n_plain_tiles, n_rows = rope_cols // tn, (n - rope_cols) // tn, m // tm
    x_spec = pl.BlockSpec((tm, k), lambda j, i: (i, 0))
    w_spec = pl.BlockSpec(memory_space=pl.ANY)
    out_spec = pl.BlockSpec((tm, tn), lambda j, i: (i, j))
    tab = pl.BlockSpec((tm, LANES), lambda j, i: (i, 0))
    roped = pl.pallas_call(
        functools.partial(_in_proj_rope_kernel, n_scaled_tiles=scaled_cols // tn, q_scale=q_scale,
                          tn=tn, n_tiles=n_rope_tiles, n_rows=n_rows),
        grid=(n_rope_tiles, n_rows),
        in_specs=[x_spec, w_spec, tab, tab, tab],
        out_specs=out_spec,
        out_shape=jax.ShapeDtypeStruct((m, rope_cols), BF16),
        scratch_shapes=_stream_scratch(1, k, tn, n_rows),
        compiler_params=_params("arbitrary", "arbitrary"),
        name="in_proj_rope",
    )(xn, w, ct, s1, s2)
    plain = pl.pallas_call(
        functools.partial(_in_proj_plain_kernel, scaled_tiles=sb_tiles, scale=sb_q_scale,
                          tn=tn, n_tiles=n_plain_tiles, n_rows=n_rows, first_col_tile=n_rope_tiles),
        grid=(n_plain_tiles, n_rows),
        in_specs=[x_spec, w_spec],
        out_specs=out_spec,
        out_shape=jax.ShapeDtypeStruct((m, n - rope_cols), BF16),
        scratch_shapes=_stream_scratch(1, k, tn, n_rows),
        compiler_params=_params("arbitrary", "arbitrary"),
        name="in_proj_plain",
    )(xn, w)
    return roped, plain


def _qk(q, k):
    return lax.dot_general(q, k, (((1,), (1,)), ((), ())), preferred_element_type=F32)


def _diff_attn_kernel(q_ref, k_ref, v_ref, lq1_ref, lk1_ref, lq2_ref, lk2_ref, g_ref, o_ref,
                      vp_ref, s_ref, m_ref, acc_ref, *, tq, lambda_init):
    tk = tq
    pair = pl.program_id(1)

    @pl.when(pair == 0)
    def _():
        vp_ref[:, :HEAD_DIM] = v_ref[...]
        vp_ref[:, HEAD_DIM:] = jnp.ones((v_ref.shape[0], HEAD_DIM), BF16)

    lane_q = lax.broadcasted_iota(jnp.int32, (tq, HEAD_DIM), 1)
    lane = lax.broadcasted_iota(jnp.int32, (2 * tq, LANES), 1)
    row = lax.broadcasted_iota(jnp.int32, (2 * tq, LANES), 0)
    row_in_tile = jnp.where(row >= tq, row - tq, row)

    class Tile:
        def __init__(self, t):
            self.t = t
            self.n_blocks = 2 * pair + t + 1
            q = q_ref[t * tq:(t + 1) * tq, :]
            zero = jnp.zeros_like(q)
            self.q_st = jnp.concatenate(
                [jnp.where(lane_q < DIFF_QK_DIM, q, zero), jnp.where(lane_q >= DIFF_QK_DIM, q, zero)], axis=0)
            m_ref[t] = jnp.full((2 * tq, LANES), -jnp.inf, F32)
            acc_ref[t] = jnp.zeros((2 * tq, 2 * HEAD_DIM), F32)

        def key_start(self, b):
            return pl.multiple_of(jnp.maximum(self.n_blocks - 1 - b, 0) * tk, tk)

        def scores(self, b, slot):
            s_ref[self.t, slot] = _qk(self.q_st, k_ref[pl.ds(self.key_start(b), tk), :])

        def softmax_sum(self, b, slot, diagonal):
            t = self.t

            def chunk(c):
                s_c = s_ref[t, slot, :, c * LANES:(c + 1) * LANES]
                return jnp.where(lane + c * LANES <= row_in_tile, s_c, -jnp.inf) if diagonal else s_c

            m_blk = chunk(0)
            for c in range(1, tk // LANES):
                m_blk = jnp.maximum(m_blk, chunk(c))
            m_old = m_ref[t]
            m_new = jnp.maximum(m_old, jnp.max(m_blk, axis=-1, keepdims=True))
            alpha = jnp.exp2(m_old - m_new)
            m_ref[t] = m_new
            p = jnp.concatenate([jnp.exp2(chunk(c) - m_new).astype(BF16) for c in range(tk // LANES)], axis=1)
            pv = jnp.dot(p, vp_ref[pl.ds(self.key_start(b), tk), :], preferred_element_type=F32)
            acc_ref[t, :, :HEAD_DIM] = alpha * acc_ref[t, :, :HEAD_DIM] + pv[:, :HEAD_DIM]
            acc_ref[t, :, HEAD_DIM:] = alpha * acc_ref[t, :, HEAD_DIM:] + pv[:, HEAD_DIM:]

        def finish(self, lam):
            acc = acc_ref[self.t]
            o1 = acc[:tq, :HEAD_DIM] / acc[:tq, HEAD_DIM:]
            o2 = acc[tq:, :HEAD_DIM] / acc[tq:, HEAD_DIM:]
            out = o1 - lam * o2
            o_ref[self.t * tq:(self.t + 1) * tq, :] = (
                (_rms(out) * g_ref[...]) * (1.0 - lambda_init)).astype(o_ref.dtype)

    tiles = (Tile(0), Tile(1))
    for tl in tiles:
        tl.scores(0, 0)
    for tl in tiles:
        tl.scores(1, 1)
    for tl in tiles:
        tl.softmax_sum(0, 0, True)

    def body(j, carry):
        b = 2 * j + 1
        for tl in tiles:
            tl.scores(b + 1, 0)
            tl.softmax_sum(b, 1, False)
        for tl in tiles:
            tl.scores(b + 2, 1)
            tl.softmax_sum(b + 1, 0, False)
        return carry

    lax.fori_loop(0, pair, body, 0)
    tiles[1].softmax_sum(2 * pair + 1, 1, False)

    lam = (jnp.exp(jnp.sum(lq1_ref[...] * lk1_ref[...], keepdims=True))
           - jnp.exp(jnp.sum(lq2_ref[...] * lk2_ref[...], keepdims=True)) + lambda_init)
    for tl in tiles:
        tl.finish(lam)


def diff_attention(qkv, lq1, lk1, lq2, lk2, subln, *, n_heads, q_col, k_col, v_col, lambda_init, tq=512):
    s = qkv[0].shape[0]
    tq = min(tq, s // 2)
    assert s % (2 * tq) == 0
    qb, kb, vb = q_col // HEAD_DIM, k_col // HEAD_DIM, v_col // HEAD_DIM
    lam_spec = pl.BlockSpec((1, DIFF_QK_DIM), lambda h, i: (0, 0))
    return pl.pallas_call(
        functools.partial(_diff_attn_kernel, tq=tq, lambda_init=lambda_init),
        grid=(n_heads, s // (2 * tq)),
        in_specs=[
            pl.BlockSpec((2 * tq, HEAD_DIM), lambda h, i: (i, qb + h)),
            pl.BlockSpec((s, HEAD_DIM), lambda h, i: (0, kb + h)),
            pl.BlockSpec((s, HEAD_DIM), lambda h, i: (0, vb + h)),
            lam_spec, lam_spec, lam_spec, lam_spec,
            pl.BlockSpec((1, HEAD_DIM), lambda h, i: (0, 0)),
        ],
        out_specs=pl.BlockSpec((2 * tq, HEAD_DIM), lambda h, i: (i, h)),
        out_shape=jax.ShapeDtypeStruct((s, n_heads * HEAD_DIM), BF16),
        scratch_shapes=[
            pltpu.VMEM((s, 2 * HEAD_DIM), BF16),
            pltpu.VMEM((2, 2, 2 * tq, tq), F32),
            pltpu.VMEM((2, 2 * tq, LANES), F32),
            pltpu.VMEM((2, 2 * tq, 2 * HEAD_DIM), F32),
        ],
        compiler_params=_params("arbitrary", "arbitrary"),
        name="diff_attention",
    )(*qkv, lq1.reshape(1, -1), lk1.reshape(1, -1), lq2.reshape(1, -1), lk2.reshape(1, -1), subln.reshape(1, -1))


def _sb_attn_kernel(q_ref, k_ref, v_ref, tri_ref, g_ref, o_ref, c_ref, acc_ref, *, tq, tk, heads):
    i = pl.program_id(1)
    tri = tri_ref[...]

    def block(h, kb, masked, c):
        cols = slice(h * HEAD_DIM, (h + 1) * HEAD_DIM)
        start = pl.multiple_of(kb * tk, tk)
        k = k_ref[pl.ds(start, tk), cols]
        v = v_ref[pl.ds(start, tk), cols]
        z = _qk(q_ref[:, cols], k)
        nz = -z
        lsm = jnp.minimum(nz, 0.0) - jnp.log2(1.0 + jnp.exp2(jnp.minimum(z, nz)))
        if masked:
            q_pos = i * tq + lax.broadcasted_iota(jnp.int32, z.shape, 0)
            k_pos = kb * tk + lax.broadcasted_iota(jnp.int32, z.shape, 1)
            strict = k_pos < q_pos
            lsm = jnp.where(strict, lsm, 0.0)
        hi = lsm.astype(BF16)
        lo = (lsm - hi.astype(F32)).astype(BF16)
        r = jnp.dot(hi, tri, preferred_element_type=F32) + jnp.dot(lo, tri, preferred_element_type=F32)
        a = jnp.exp2(z + r + c)
        if masked:
            a = jnp.where(strict, a, 0.0)
        return jnp.dot(a.astype(BF16), v, preferred_element_type=F32), c + r[:, 0:1]

    n_diag = tq // tk
    kb_top = (i + 1) * n_diag - 1

    def head_blocks(with_left):
        for h in range(heads):
            c = jnp.zeros((tq, 1), F32)
            acc = None
            for d in range(n_diag + int(with_left)):
                pv, c = block(h, kb_top - d, d < n_diag, c)
                acc = pv if acc is None else acc + pv
            acc_ref[h] = acc
            c_ref[h] = c

    pl.when(i == 0)(functools.partial(head_blocks, False))
    pl.when(i > 0)(functools.partial(head_blocks, True))

    def cond(carry):
        kb, cm = carry
        return jnp.logical_and(kb >= 0, cm > -SB_EXIT * LOG2_E)

    def body(carry):
        kb, _ = carry
        for h in range(heads):
            pv, c = block(h, kb, False, c_ref[h])
            acc_ref[h] += pv
            c_ref[h] = c
        return kb - 1, jnp.max(c_ref[...])

    lax.while_loop(cond, body, (kb_top - n_diag - 1, jnp.max(c_ref[...])))
    for h in range(heads):
        o_ref[:, h * HEAD_DIM:(h + 1) * HEAD_DIM] = (_rms(acc_ref[h]) * g_ref[...]).astype(o_ref.dtype)


def sb_attention(qkv, gain, *, n_heads, q_col, k_col, v_col, tq=256, tk=256, heads=4):
    s = qkv[0].shape[0]
    tq, tk, heads = min(tq, s), min(tk, s), min(heads, n_heads)
    width = heads * HEAD_DIM
    assert tq % tk == 0 and n_heads % heads == 0 and all(c % width == 0 for c in (q_col, k_col, v_col))
    qb, kb, vb = q_col // width, k_col // width, v_col // width
    tri = (lax.broadcasted_iota(jnp.int32, (tk, tk), 0) >= lax.broadcasted_iota(jnp.int32, (tk, tk), 1)).astype(BF16)
    return pl.pallas_call(
        functools.partial(_sb_attn_kernel, tq=tq, tk=tk, heads=heads),
        grid=(n_heads // heads, s // tq),
        in_specs=[
            pl.BlockSpec((tq, width), lambda g, i: (i, qb + g)),
            pl.BlockSpec((s, width), lambda g, i: (0, kb + g)),
            pl.BlockSpec((s, width), lambda g, i: (0, vb + g)),
            pl.BlockSpec((tk, tk), lambda g, i: (0, 0)),
            pl.BlockSpec((1, HEAD_DIM), lambda g, i: (0, 0)),
        ],
        out_specs=pl.BlockSpec((tq, width), lambda g, i: (i, g)),
        out_shape=jax.ShapeDtypeStruct((s, n_heads * HEAD_DIM), BF16),
        scratch_shapes=[pltpu.VMEM((heads, tq, 1), F32), pltpu.VMEM((heads, tq, HEAD_DIM), F32)],
        compiler_params=_params("parallel", "arbitrary"),
        name="sb_attention",
    )(*qkv, tri, gain.reshape(1, -1))


def _mem_kv_kernel(mem_ref, g_ref, w_ref, o_ref):
    mn = (_rms(mem_ref[...]) * g_ref[...]).astype(BF16)
    o_ref[...] = jnp.dot(mn, w_ref[...], preferred_element_type=F32).astype(o_ref.dtype)


def mem_kv(mem, gain, w_kv):
    mt, d = mem.shape
    n = w_kv.shape[1]
    return pl.pallas_call(
        _mem_kv_kernel,
        grid=(1,),
        in_specs=[pl.BlockSpec((mt, d), lambda i: (0, 0)), pl.BlockSpec((1, d), lambda i: (0, 0)),
                  pl.BlockSpec((d, n), lambda i: (0, 0))],
        out_specs=pl.BlockSpec((mt, n), lambda i: (0, 0)),
        out_shape=jax.ShapeDtypeStruct((mt, n), BF16),
        compiler_params=_params("arbitrary"),
        name="mem_kv",
    )(mem, gain.reshape(1, d), w_kv)


def _xattn_kernel(h_ref, y_ref, gmix_ref, gpre_ref, wq_ref, kv_ref, wo_ref, gpost_ref, gnext_ref, hout_ref, hn_ref, *,
                  scale):
    h = h_ref[...] + _rms(y_ref[...]) * gmix_ref[...]
    hn = (_rms(h) * gpre_ref[...]).astype(BF16)
    q = jnp.dot(hn, wq_ref[...], preferred_element_type=F32).astype(BF16)
    width = XATTN_HEADS * HEAD_DIM
    outs = []
    for hd in range(XATTN_HEADS):
        cols = slice(hd * HEAD_DIM, (hd + 1) * HEAD_DIM)
        k = kv_ref[:, cols]
        v = kv_ref[:, width + hd * HEAD_DIM:width + (hd + 1) * HEAD_DIM]
        sc = _qk(q[:, cols], k) * scale
        sc = sc - jnp.max(sc, axis=-1, keepdims=True)
        e = jnp.exp(sc)
        p = e / jnp.sum(e, axis=-1, keepdims=True)
        outs.append(jnp.dot(p.astype(BF16), v, preferred_element_type=F32).astype(BF16))
    o = jnp.concatenate(outs, axis=-1)
    y = jnp.dot(o, wo_ref[...], preferred_element_type=F32)
    h2 = h + _rms(y) * gpost_ref[...]
    hout_ref[...] = h2
    hn_ref[...] = (_rms(h2) * gnext_ref[...]).astype(hn_ref.dtype)


def xattn_block(h, y_mix, gmix, gpre, wq, kv, wo, gpost, gnext, *, tm=256):
    m, d = h.shape
    tm = min(tm, m)
    width = wq.shape[1]
    row = pl.BlockSpec((tm, d), lambda i: (i, 0))
    vec = pl.BlockSpec((1, d), lambda i: (0, 0))
    return pl.pallas_call(
        functools.partial(_xattn_kernel, scale=HEAD_DIM ** -0.5),
        grid=(m // tm,),
        in_specs=[row, row, vec, vec, pl.BlockSpec((d, width), lambda i: (0, 0)),
                  pl.BlockSpec(kv.shape, lambda i: (0, 0)), pl.BlockSpec((width, d), lambda i: (0, 0)), vec, vec],
        out_specs=[row, row],
        out_shape=[jax.ShapeDtypeStruct((m, d), F32), jax.ShapeDtypeStruct((m, d), BF16)],
        compiler_params=_params("parallel"),
        name="xattn_block",
    )(h, y_mix, gmix.reshape(1, d), gpre.reshape(1, d), wq, kv, wo, gpost.reshape(1, d), gnext.reshape(1, d))


def kernel(x, mem, positions, ffn1_norm_pre, ffn1_norm_post, ffn1_w_gate, ffn1_w_up, ffn1_w_down, mix_norm_pre, mix_norm_post, w_in, w_out, lambda_q1, lambda_k1, lambda_q2, lambda_k2, diff_subln, sb_norm, xattn_norm_pre, xattn_norm_post, mem_norm, xattn_w_q, xattn_w_kv, xattn_w_o, ffn2_norm_pre, ffn2_norm_post, ffn2_w_gate, ffn2_w_up, ffn2_w_down):
    b, s, d = x.shape
    depth = w_in.shape[0]
    in_width = w_in.shape[2]
    group = in_width // 6
    n_heads = group // HEAD_DIM
    outs = []
    for bi in range(b):
        h = x[bi]
        pos_col = positions[bi].astype(F32).reshape(s, 1)
        ct, s1, s2 = rope_tables(pos_col)
        hn = prenorm(h, ffn1_norm_pre[0])
        for l in range(depth):
            lambda_init = 0.8 - 0.6 * math.exp(-0.3 * l)
            cast = lambda w: w[l].astype(BF16)
            act = ffn_up(hn, ffn1_w_gate[l], ffn1_w_up[l])
            y = matmul_kblocked(act, ffn1_w_down[l])
            h, hn = post_pre(h, y, ffn1_norm_post[l], mix_norm_pre[l], 0.5)
            qk, rest = in_proj(hn, w_in[l], ct, s1, s2, rope_cols=2 * group, scaled_cols=group,
                               q_scale=DIFF_QK_DIM ** -0.5 * LOG2_E, sb_q_cols=(3 * group, 4 * group),
                               sb_q_scale=HEAD_DIM ** -0.5 * LOG2_E)
            diff_out = diff_attention((qk, qk, rest), lambda_q1[l], lambda_k1[l], lambda_q2[l], lambda_k2[l],
                                      diff_subln[l], n_heads=n_heads, q_col=0, k_col=group, v_col=0,
                                      lambda_init=lambda_init)
            sb_out = sb_attention((rest, rest, rest), sb_norm[l], n_heads=n_heads, q_col=group, k_col=2 * group,
                                  v_col=3 * group)
            y = out_proj(diff_out, sb_out, w_out[l])
            kv = mem_kv(mem[bi], mem_norm[l], cast(xattn_w_kv))
            h, hn = xattn_block(h, y, mix_norm_post[l], xattn_norm_pre[l], cast(xattn_w_q), kv, cast(xattn_w_o),
                                xattn_norm_post[l], ffn2_norm_pre[l])
            act = ffn_up(hn, ffn2_w_gate[l], ffn2_w_up[l])
            y = matmul_kblocked(act, ffn2_w_down[l])
            if l + 1 < depth:
                h, hn = post_pre(h, y, ffn2_norm_post[l], ffn1_norm_pre[l + 1], 0.5)
            else:
                h = post_pre(h, y, ffn2_norm_post[l], None, 0.5)
        outs.append(h)
    return jnp.stack(outs, axis=0)
```
